```python
import jax, jax.numpy as jnp
from jax import lax
import numpy as np

D_MODEL = 1024
BATCH = 8
SEQ = 8192
DEPTH = 1

HEAD_DIM = 64
N_Q_HEADS = 8
N_KV_HEADS = 2
GROUP = N_Q_HEADS // N_KV_HEADS
ATTN_WIDTH = N_Q_HEADS * HEAD_DIM
KV_WIDTH = N_KV_HEADS * HEAD_DIM
WINDOW = 128
BLOCK = 128
ROT_DIM = HEAD_DIM // 4
ROPE_THETA = 500000.0
CONV_WIDTH = D_MODEL - ATTN_WIDTH
CONV_K = 3
MIX_WIDTH = ATTN_WIDTH + CONV_WIDTH
IN_SPLITS = [ATTN_WIDTH, KV_WIDTH, KV_WIDTH, ATTN_WIDTH,
             CONV_WIDTH, CONV_WIDTH, CONV_WIDTH, CONV_WIDTH]
IN_WIDTH = sum(IN_SPLITS)
EPS = 1e-5

kernel_name = "hybrid_swa_sink_shortconv_block"


def rms_norm(x, g):
    xf = x.astype(jnp.float32)
    y = xf * lax.rsqrt(jnp.mean(xf * xf, axis=-1, keepdims=True) + EPS)
    return (y * g.astype(jnp.float32)).astype(x.dtype)


def partial_rope(t, pos):
    half = ROT_DIM // 2
    inv_freq = ROPE_THETA ** (-jnp.arange(0, ROT_DIM, 2, dtype=jnp.float32) / ROT_DIM)
    ang = pos.astype(jnp.float32)[:, None] * inv_freq[None, :]
    cos = jnp.cos(ang)[None, :, None, :]
    sin = jnp.sin(ang)[None, :, None, :]
    rot = t[..., :ROT_DIM].astype(jnp.float32)
    x1, x2 = rot[..., :half], rot[..., half:]
    r = jnp.concatenate([x1 * cos - x2 * sin, x2 * cos + x1 * sin], axis=-1)
    return jnp.concatenate([r.astype(t.dtype), t[..., ROT_DIM:]], axis=-1)


def sliding_window_sink_attention(q, k, v, sinks):
    B, S, _, D = q.shape
    nb = S // BLOCK
    qb = q.reshape(B, nb, BLOCK, N_KV_HEADS, GROUP, D)
    kb = k.reshape(B, nb, BLOCK, N_KV_HEADS, D)
    vb = v.reshape(B, nb, BLOCK, N_KV_HEADS, D)
    pad = ((0, 0), (1, 0), (0, 0), (0, 0), (0, 0))
    kk = jnp.concatenate([jnp.pad(kb, pad)[:, :-1], kb], axis=2)
    vv = jnp.concatenate([jnp.pad(vb, pad)[:, :-1], vb], axis=2)
    scale = 1.0 / np.sqrt(D)
    s = jnp.einsum('bnqkgd,bnskd->bnkgqs', qb, kk,
                   preferred_element_type=jnp.float32) * scale
    qi = jnp.arange(BLOCK)[:, None]
    kj = jnp.arange(2 * BLOCK)[None, :]
    delta = qi + BLOCK - kj
    band = (delta >= 0) & (delta < WINDOW)
    valid = (jnp.arange(nb)[:, None] * BLOCK + kj - BLOCK) >= 0
    mask = (band[None] & valid[:, None, :])[None, :, None, None]
    s = jnp.where(mask, s, -jnp.inf)
    sink = sinks.astype(jnp.float32).reshape(N_KV_HEADS, GROUP)[None, None, :, :, None, None]
    m = jnp.maximum(jnp.max(s, axis=-1, keepdims=True), sink)
    p = jnp.exp(s - m)
    denom = jnp.sum(p, axis=-1, keepdims=True) + jnp.exp(sink - m)
    prob = (p / denom).astype(v.dtype)
    o = jnp.einsum('bnkgqs,bnskd->bnqkgd', prob, vv,
                   preferred_element_type=jnp.float32)
    return o.reshape(B, S, N_Q_HEADS * D).astype(q.dtype)


def causal_short_conv(u, w):
    S = u.shape[1]
    up = jnp.pad(u, ((0, 0), (CONV_K - 1, 0), (0, 0)))
    y = w[0] * up[:, 0:S]
    for j in range(1, CONV_K):
        y = y + w[j] * up[:, j:j + S]
    return y


def setup_inputs(seed: int = 0) -> dict:
    key = jax.random.key(seed)
    ks = jax.random.split(key, 8)
    f32 = jnp.float32
    x = jax.random.normal(ks[0], (BATCH, SEQ, D_MODEL), f32)
    norm_g = 1.0 + 0.02 * jax.random.normal(ks[1], (D_MODEL,), f32)
    w_in = jax.random.normal(ks[2], (D_MODEL, IN_WIDTH), f32) * D_MODEL ** -0.5
    sinks = 0.5 * jax.random.normal(ks[3], (N_Q_HEADS,), f32)
    conv_w = jax.random.normal(ks[4], (CONV_K, CONV_WIDTH), f32) * CONV_K ** -0.5
    w_out = jax.random.normal(ks[5], (MIX_WIDTH, D_MODEL), f32) * MIX_WIDTH ** -0.5
    final_g = 1.0 + 0.02 * jax.random.normal(ks[6], (D_MODEL,), f32)
    return {"x": x, "norm_g": norm_g, "w_in": w_in, "sinks": sinks,
            "conv_w": conv_w, "w_out": w_out, "final_g": final_g}


def reference(x, norm_g, w_in, sinks, conv_w, w_out, final_g):
    B, S, _ = x.shape
    pos = jnp.arange(S, dtype=jnp.int32)
    h = x
    for _ in range(DEPTH):
        xn = rms_norm(h, norm_g)
        proj = jnp.einsum('bsd,de->bse', xn, w_in)
        offs = list(np.cumsum(IN_SPLITS)[:-1])
        q, k, v, g_attn, b_gate, c_gate, h_in, g_conv = jnp.split(proj, offs, axis=-1)
        q = partial_rope(q.reshape(B, S, N_Q_HEADS, HEAD_DIM), pos)
        k = partial_rope(k.reshape(B, S, N_KV_HEADS, HEAD_DIM), pos)
        v = v.reshape(B, S, N_KV_HEADS, HEAD_DIM)
        attn = sliding_window_sink_attention(q, k, v, sinks)
        y_attn = attn * jax.nn.silu(g_attn)
        y_conv = b_gate * causal_short_conv(c_gate * h_in, conv_w)
        y_conv = y_conv * jax.nn.silu(g_conv)
        mix = jnp.concatenate([y_attn, y_conv], axis=-1)
        h = h + jnp.einsum('bse,ed->bsd', mix, w_out)
    return rms_norm(h, final_g)
```

```python
import functools

import jax
import jax.numpy as jnp
import numpy as np
from jax import lax
from jax.experimental import pallas as pl
from jax.experimental.pallas import tpu as pltpu

D_MODEL = 1024
HEAD_DIM = 64
N_Q_HEADS = 8
N_KV_HEADS = 2
GROUP = N_Q_HEADS // N_KV_HEADS
ATTN_WIDTH = N_Q_HEADS * HEAD_DIM
KV_WIDTH = N_KV_HEADS * HEAD_DIM
BLOCK = 128
ROT_DIM = HEAD_DIM // 4
ROPE_THETA = 500000.0
CONV_WIDTH = D_MODEL - ATTN_WIDTH
CONV_K = 3
EPS = 1e-5

LANES = 128
CONV_PAD = 8
SEQ_TILE = 512
VMEM_LIMIT_BYTES = 56 * 1024 * 1024

OFF_Q = 0
OFF_KV = OFF_Q + ATTN_WIDTH
OFF_GA = OFF_KV + 2 * KV_WIDTH
OFF_B = OFF_GA + ATTN_WIDTH
OFF_C = OFF_B + CONV_WIDTH
OFF_H = OFF_C + CONV_WIDTH
OFF_GC = OFF_H + CONV_WIDTH


def _rope_tables(seq):
    half = ROT_DIM // 2
    inv_freq = ROPE_THETA ** (-jnp.arange(0, ROT_DIM, 2, dtype=jnp.float32) / ROT_DIM)
    ang = jnp.arange(seq, dtype=jnp.int32).astype(jnp.float32)[:, None] * inv_freq[None, :]
    cos, sin = jnp.cos(ang), jnp.sin(ang)
    ones = jnp.ones((seq, HEAD_DIM - ROT_DIM), jnp.float32)
    zeros_h = jnp.zeros((seq, half), jnp.float32)
    zeros_t = jnp.zeros((seq, HEAD_DIM - ROT_DIM), jnp.float32)
    cos_t = jnp.concatenate([cos, cos, ones], axis=1)
    sin_lo = jnp.concatenate([-sin, zeros_h, zeros_t], axis=1)
    sin_hi = jnp.concatenate([zeros_h, sin, zeros_t], axis=1)
    rep = LANES // HEAD_DIM
    return tuple(jnp.tile(t, (1, rep)) for t in (cos_t, sin_lo, sin_hi))


def _rms_norm(x, g):
    ms = jnp.mean(x * x, axis=-1, keepdims=True)
    return x * lax.rsqrt(ms + EPS) * g


def _silu(x):
    return x * (1.0 / (1.0 + jnp.exp(-x)))


def _layer_kernel(sinks_ref, x_ref, ng_ref, fg_ref, win_ref, wout_ref, convw_ref,
                  cos_ref, sinlo_ref, sinhi_ref, out_ref,
                  q_buf, k_buf, v_buf, u_buf, mix_buf):
    tile = x_ref.shape[0]
    n_blocks = tile // BLOCK
    j = pl.program_id(1)
    f32, bf16 = jnp.float32, jnp.bfloat16

    @pl.when(j == 0)
    def _():
        k_buf[0:BLOCK, :] = jnp.zeros((BLOCK, KV_WIDTH), bf16)
        v_buf[0:BLOCK, :] = jnp.zeros((BLOCK, KV_WIDTH), bf16)
        u_buf[0:CONV_PAD, :] = jnp.zeros((CONV_PAD, CONV_WIDTH), f32)

    x = x_ref[...]
    xn = _rms_norm(x, ng_ref[...]).astype(bf16)

    def proj(off, width):
        return jnp.dot(xn, win_ref[:, off:off + width], preferred_element_type=f32)

    cos_t, sin_lo, sin_hi = cos_ref[...], sinlo_ref[...], sinhi_ref[...]

    def rope(t):
        return (t * cos_t + pltpu.roll(t, LANES - ROT_DIM // 2, 1) * sin_lo
                + pltpu.roll(t, ROT_DIM // 2, 1) * sin_hi)

    q = proj(OFF_Q, ATTN_WIDTH)
    scale = HEAD_DIM ** -0.5
    for c in range(ATTN_WIDTH // LANES):
        qc = rope(q[:, c * LANES:(c + 1) * LANES]) * scale
        q_buf[:, c * LANES:(c + 1) * LANES] = qc.astype(bf16)
    kv = proj(OFF_KV, 2 * KV_WIDTH)
    k_buf[BLOCK:BLOCK + tile, :] = rope(kv[:, 0:KV_WIDTH]).astype(bf16)
    v_buf[BLOCK:BLOCK + tile, :] = kv[:, KV_WIDTH:2 * KV_WIDTH].astype(bf16)

    g_attn = _silu(proj(OFF_GA, ATTN_WIDTH))

    qi = lax.broadcasted_iota(jnp.int32, (BLOCK, 2 * BLOCK), 0)
    kj = lax.broadcasted_iota(jnp.int32, (BLOCK, 2 * BLOCK), 1)
    in_cur = (kj >= BLOCK) & (kj - BLOCK <= qi)
    in_prev = (kj < BLOCK) & (kj > qi)
    neg_inf = jnp.float32(-jnp.inf)
    bias_mid = jnp.where(in_cur | in_prev, 0.0, neg_inf)
    bias_first = jnp.where(in_cur | (in_prev & (j > 0)), 0.0, neg_inf)

    for n in range(n_blocks):
        rows = slice(n * BLOCK, (n + 1) * BLOCK)
        bias = bias_first if n == 0 else bias_mid
        bias4 = jnp.concatenate([bias] * GROUP, axis=0)
        for g in range(N_KV_HEADS):
            kk = k_buf[n * BLOCK:(n + 2) * BLOCK, g * HEAD_DIM:(g + 1) * HEAD_DIM]
            vv = v_buf[n * BLOCK:(n + 2) * BLOCK, g * HEAD_DIM:(g + 1) * HEAD_DIM]
            heads = [g * GROUP + h for h in range(GROUP)]
            qg = jnp.concatenate(
                [q_buf[rows, hd * HEAD_DIM:(hd + 1) * HEAD_DIM] for hd in heads], axis=0)
            s = lax.dot_general(qg, kk, (((1,), (1,)), ((), ())),
                                preferred_element_type=f32) + bias4
            sink = jnp.concatenate(
                [jnp.full((BLOCK, 1), sinks_ref[hd], f32) for hd in heads], axis=0)
            m = jnp.maximum(jnp.max(s, axis=-1, keepdims=True), sink)
            p = jnp.exp(s - m)
            denom = jnp.sum(p, axis=-1, keepdims=True) + jnp.exp(sink - m)
            o = jnp.dot(p.astype(bf16), vv, preferred_element_type=f32) / denom
            attn = jnp.concatenate(
                [o[h * BLOCK:(h + 1) * BLOCK, :] for h in range(GROUP)], axis=1)
            cols = slice(g * GROUP * HEAD_DIM, (g + 1) * GROUP * HEAD_DIM)
            mix_buf[rows, cols] = (attn * g_attn[rows, cols]).astype(bf16)

    u = proj(OFF_C, CONV_WIDTH) * proj(OFF_H, CONV_WIDTH)
    u_buf[CONV_PAD:CONV_PAD + tile, :] = u
    conv = (convw_ref[0:1, :] * u_buf[CONV_PAD - 2:CONV_PAD - 2 + tile, :]
            + convw_ref[1:2, :] * u_buf[CONV_PAD - 1:CONV_PAD - 1 + tile, :]
            + convw_ref[2:3, :] * u)
    y_conv = proj(OFF_B, CONV_WIDTH) * conv * _silu(proj(OFF_GC, CONV_WIDTH))
    mix_buf[:, ATTN_WIDTH:ATTN_WIDTH + CONV_WIDTH] = y_conv.astype(bf16)

    u_buf[0:CONV_PAD, :] = u_buf[tile:tile + CONV_PAD, :]
    k_buf[0:BLOCK, :] = k_buf[tile:tile + BLOCK, :]
    v_buf[0:BLOCK, :] = v_buf[tile:tile + BLOCK, :]

    h = x + jnp.dot(mix_buf[...], wout_ref[...], preferred_element_type=f32)
    out_ref[...] = _rms_norm(h, fg_ref[...])


@jax.jit
def kernel(x, norm_g, w_in, sinks, conv_w, w_out, final_g):
    batch, seq, d_model = x.shape
    tile = SEQ_TILE
    assert d_model == D_MODEL and seq % tile == 0 and tile % BLOCK == 0
    in_width = w_in.shape[1]
    cos_t, sin_lo, sin_hi = _rope_tables(seq)

    grid_spec = pltpu.PrefetchScalarGridSpec(
        num_scalar_prefetch=1,
        grid=(batch, seq // tile),
        in_specs=[
            pl.BlockSpec((None, tile, D_MODEL), lambda b, j, s: (b, j, 0)),
            pl.BlockSpec((1, D_MODEL), lambda b, j, s: (0, 0)),
            pl.BlockSpec((1, D_MODEL), lambda b, j, s: (0, 0)),
            pl.BlockSpec((D_MODEL, in_width), lambda b, j, s: (0, 0)),
            pl.BlockSpec((D_MODEL, D_MODEL), lambda b, j, s: (0, 0)),
            pl.BlockSpec((CONV_K, CONV_WIDTH), lambda b, j, s: (0, 0)),
            pl.BlockSpec((tile, LANES), lambda b, j, s: (j, 0)),
            pl.BlockSpec((tile, LANES), lambda b, j, s: (j, 0)),
            pl.BlockSpec((tile, LANES), lambda b, j, s: (j, 0)),
        ],
        out_specs=pl.BlockSpec((None, tile, D_MODEL), lambda b, j, s: (b, j, 0)),
        scratch_shapes=[
            pltpu.VMEM((tile, ATTN_WIDTH), jnp.bfloat16),
            pltpu.VMEM((BLOCK + tile, KV_WIDTH), jnp.bfloat16),
            pltpu.VMEM((BLOCK + tile, KV_WIDTH), jnp.bfloat16),
            pltpu.VMEM((CONV_PAD + tile, CONV_WIDTH), jnp.float32),
            pltpu.VMEM((tile, D_MODEL), jnp.bfloat16),
        ],
    )
    return pl.pallas_call(
        _layer_kernel,
        grid_spec=grid_spec,
        out_shape=jax.ShapeDtypeStruct(x.shape, x.dtype),
        compiler_params=pltpu.CompilerParams(
            dimension_semantics=("arbitrary", "arbitrary"),
            vmem_limit_bytes=VMEM_LIMIT_BYTES),
        name="hybrid_layer",
    )(sinks.astype(jnp.float32), x, norm_g.reshape(1, D_MODEL), final_g.reshape(1, D_MODEL),
      w_in.astype(jnp.bfloat16), w_out.astype(jnp.bfloat16), conv_w,
      cos_t, sin_lo, sin_hi)
```

```python
import jax
import jax.numpy as jnp
from jax import lax
from jax.experimental import pallas as pl
from jax.experimental.pallas import tpu as pltpu

D_MODEL = 1024
HEAD_DIM = 64
N_Q_HEADS = 8
N_KV_HEADS = 2
GROUP = N_Q_HEADS // N_KV_HEADS
ATTN_WIDTH = N_Q_HEADS * HEAD_DIM
KV_WIDTH = N_KV_HEADS * HEAD_DIM
BLOCK = 128
ROT_DIM = HEAD_DIM // 4
ROPE_THETA = 500000.0
CONV_WIDTH = D_MODEL - ATTN_WIDTH
CONV_K = 3
EPS = 1e-5

LANES = 128
BF16_ROWS = 16
CONV_PAD = 8
SEQ_TILE = 512
VMEM_LIMIT_BYTES = 56 * 1024 * 1024

OFF_Q = 0
OFF_KV = OFF_Q + ATTN_WIDTH
OFF_GA = OFF_KV + 2 * KV_WIDTH
OFF_B = OFF_GA + ATTN_WIDTH
OFF_C = OFF_B + CONV_WIDTH
OFF_H = OFF_C + CONV_WIDTH
OFF_GC = OFF_H + CONV_WIDTH


def _rope_tables(seq):
    half = ROT_DIM // 2
    inv_freq = ROPE_THETA ** (-jnp.arange(0, ROT_DIM, 2, dtype=jnp.float32) / ROT_DIM)
    ang = jnp.arange(seq, dtype=jnp.int32).astype(jnp.float32)[:, None] * inv_freq[None, :]
    cos, sin = jnp.cos(ang), jnp.sin(ang)
    ones = jnp.ones((seq, HEAD_DIM - ROT_DIM), jnp.float32)
    zeros_h = jnp.zeros((seq, half), jnp.float32)
    zeros_t = jnp.zeros((seq, HEAD_DIM - ROT_DIM), jnp.float32)
    cos_t = jnp.concatenate([cos, cos, ones], axis=1)
    sin_lo = jnp.concatenate([-sin, zeros_h, zeros_t], axis=1)
    sin_hi = jnp.concatenate([zeros_h, sin, zeros_t], axis=1)
    rep = LANES // HEAD_DIM
    return tuple(jnp.tile(t, (1, rep)) for t in (cos_t, sin_lo, sin_hi))


def _rms_norm(x, g):
    ms = jnp.mean(x * x, axis=-1, keepdims=True)
    return x * lax.rsqrt(ms + EPS) * g


def _silu(x):
    return x * (1.0 / (1.0 + jnp.exp(-x)))


def _layer_kernel(sinks_ref, x_ref, ng_ref, fg_ref, win_ref, wout_ref, convw_ref,
                  cos_ref, sinlo_ref, sinhi_ref, out_ref,
                  q_buf, kz_buf, vx_buf, u_buf, mix_buf):
    tile = x_ref.shape[0]
    n_blocks = tile // BLOCK
    j = pl.program_id(1)
    f32, bf16 = jnp.float32, jnp.bfloat16

    @pl.when(j == 0)
    def _():
        kz_buf[:, 0:BLOCK, :] = jnp.zeros((2 * N_KV_HEADS, BLOCK, LANES), bf16)
        vx_buf[:, 0:BLOCK, :] = jnp.ones((N_KV_HEADS, BLOCK, 2 * LANES), bf16)
        u_buf[0:CONV_PAD, :] = jnp.zeros((CONV_PAD, CONV_WIDTH), f32)

    x = x_ref[...]
    xn = _rms_norm(x, ng_ref[...]).astype(bf16)

    def proj(off, width):
        return jnp.dot(xn, win_ref[:, off:off + width], preferred_element_type=f32)

    cos_t, sin_lo, sin_hi = cos_ref[...], sinlo_ref[...], sinhi_ref[...]

    def rope(t):
        return (t * cos_t + pltpu.roll(t, LANES - ROT_DIM // 2, 1) * sin_lo
                + pltpu.roll(t, ROT_DIM // 2, 1) * sin_hi)

    q = proj(OFF_Q, ATTN_WIDTH)
    scale = HEAD_DIM ** -0.5
    for c in range(ATTN_WIDTH // LANES):
        qc = rope(q[:, c * LANES:(c + 1) * LANES]) * scale
        q_buf[:, c * LANES:(c + 1) * LANES] = qc.astype(bf16)

    kv = proj(OFF_KV, 2 * KV_WIDTH)
    low_half = lax.broadcasted_iota(jnp.int32, (tile, LANES), 1) < HEAD_DIM
    new_rows = slice(BLOCK, BLOCK + tile)
    kr = rope(kv[:, 0:KV_WIDTH])
    kr_sw = pltpu.roll(kr, HEAD_DIM, 1)
    kz_buf[0, new_rows, :] = jnp.where(low_half, kr, 0.0).astype(bf16)
    kz_buf[1, new_rows, :] = jnp.where(low_half, 0.0, kr_sw).astype(bf16)
    kz_buf[2, new_rows, :] = jnp.where(low_half, kr_sw, 0.0).astype(bf16)
    kz_buf[3, new_rows, :] = jnp.where(low_half, 0.0, kr).astype(bf16)
    vc = kv[:, KV_WIDTH:2 * KV_WIDTH]
    vc_sw = pltpu.roll(vc, HEAD_DIM, 1)
    vx_buf[0, new_rows, 0:LANES] = jnp.where(low_half, vc, 1.0).astype(bf16)
    vx_buf[0, new_rows, LANES:2 * LANES] = jnp.where(low_half, 1.0, vc_sw).astype(bf16)
    vx_buf[1, new_rows, 0:LANES] = jnp.where(low_half, vc_sw, 1.0).astype(bf16)
    vx_buf[1, new_rows, LANES:2 * LANES] = jnp.where(low_half, 1.0, vc).astype(bf16)

    g_attn = _silu(proj(OFF_GA, ATTN_WIDTH))

    qi = lax.broadcasted_iota(jnp.int32, (BLOCK, BLOCK), 0)
    kj = lax.broadcasted_iota(jnp.int32, (BLOCK, BLOCK), 1)
    neg_inf = jnp.float32(-jnp.inf)
    bias_cur = jnp.where(kj <= qi, 0.0, neg_inf)
    bias_prev_mid = jnp.where(kj > qi, 0.0, neg_inf)
    bias_prev_first = jnp.where((kj > qi) & (j > 0), 0.0, neg_inf)
    sink_slot = kj == 0
    low_half_blk = kj < HEAD_DIM
    top_r = lax.broadcasted_iota(jnp.int32, (BF16_ROWS, 2 * LANES), 0)
    top_c = lax.broadcasted_iota(jnp.int32, (BF16_ROWS, 2 * LANES), 1)
    sink_value_mask = (top_r == 0) & ((top_c < HEAD_DIM) | (top_c >= 2 * LANES - HEAD_DIM))
    nt_dims = (((1,), (1,)), ((), ()))

    for n in range(n_blocks):
        rows = slice(n * BLOCK, (n + 1) * BLOCK)
        keys = slice(n * BLOCK, (n + 2) * BLOCK)
        bias_prev = bias_prev_first if n == 0 else bias_prev_mid
        for g in range(N_KV_HEADS):
            base = g * GROUP * HEAD_DIM
            q2 = jnp.concatenate(
                [q_buf[rows, base:base + LANES], q_buf[rows, base + LANES:base + 2 * LANES]], axis=0)
            vx = vx_buf[g, keys, :]
            vx = jnp.concatenate(
                [jnp.where(sink_value_mask, 0.0, vx[0:BF16_ROWS]).astype(bf16), vx[BF16_ROWS:]], axis=0)
            pv = []
            for half in range(2):
                s = lax.dot_general(q2, kz_buf[2 * g + half, keys, :], nt_dims,
                                    preferred_element_type=f32)
                p = []
                for c in range(2):
                    sink = sinks_ref[g * GROUP + 2 * c + half]
                    sc = s[c * BLOCK:(c + 1) * BLOCK]
                    s_prev = jnp.where(sink_slot, sink, sc[:, 0:BLOCK] + bias_prev)
                    s_cur = sc[:, BLOCK:2 * BLOCK] + bias_cur
                    m = jnp.max(jnp.maximum(s_prev, s_cur), axis=-1, keepdims=True)
                    p.append(jnp.concatenate(
                        [jnp.exp(s_prev - m), jnp.exp(s_cur - m)], axis=1).astype(bf16))
                pv.append(jnp.dot(jnp.concatenate(p, axis=0), vx, preferred_element_type=f32))
            for c in range(2):
                blk = slice(c * BLOCK, (c + 1) * BLOCK)
                num = jnp.where(low_half_blk, pv[0][blk, 0:LANES], pv[1][blk, LANES:2 * LANES])
                den = jnp.where(low_half_blk, pv[0][blk, LANES:2 * LANES], pv[1][blk, 0:LANES])
                cols = slice(base + c * LANES, base + (c + 1) * LANES)
                mix_buf[rows, cols] = (num / den * g_attn[rows, cols]).astype(bf16)

    u = proj(OFF_C, CONV_WIDTH) * proj(OFF_H, CONV_WIDTH)
    u_buf[CONV_PAD:CONV_PAD + tile, :] = u
    conv = (convw_ref[0:1, :] * u_buf[CONV_PAD - 2:CONV_PAD - 2 + tile, :]
            + convw_ref[1:2, :] * u_buf[CONV_PAD - 1:CONV_PAD - 1 + tile, :]
            + convw_ref[2:3, :] * u)
    y_conv = proj(OFF_B, CONV_WIDTH) * conv * _silu(proj(OFF_GC, CONV_WIDTH))
    mix_buf[:, ATTN_WIDTH:ATTN_WIDTH + CONV_WIDTH] = y_conv.astype(bf16)

    u_buf[0:CONV_PAD, :] = u_buf[tile:tile + CONV_PAD, :]
    kz_buf[:, 0:BLOCK, :] = kz_buf[:, tile:tile + BLOCK, :]
    vx_buf[:, 0:BLOCK, :] = vx_buf[:, tile:tile + BLOCK, :]

    h = x + jnp.dot(mix_buf[...], wout_ref[...], preferred_element_type=f32)
    out_ref[...] = _rms_norm(h, fg_ref[...])


@jax.jit
def kernel(x, norm_g, w_in, sinks, conv_w, w_out, final_g):
    batch, seq, d_model = x.shape
    tile = SEQ_TILE
    assert d_model == D_MODEL and seq % tile == 0 and tile % BLOCK == 0
    in_width = w_in.shape[1]
    cos_t, sin_lo, sin_hi = _rope_tables(seq)

    grid_spec = pltpu.PrefetchScalarGridSpec(
        num_scalar_prefetch=1,
        grid=(batch, seq // tile),
        in_specs=[
            pl.BlockSpec((None, tile, D_MODEL), lambda b, j, s: (b, j, 0)),
            pl.BlockSpec((1, D_MODEL), lambda b, j, s: (0, 0)),
            pl.BlockSpec((1, D_MODEL), lambda b, j, s: (0, 0)),
            pl.BlockSpec((D_MODEL, in_width), lambda b, j, s: (0, 0)),
            pl.BlockSpec((D_MODEL, D_MODEL), lambda b, j, s: (0, 0)),
            pl.BlockSpec((CONV_K, CONV_WIDTH), lambda b, j, s: (0, 0)),
            pl.BlockSpec((tile, LANES), lambda b, j, s: (j, 0)),
            pl.BlockSpec((tile, LANES), lambda b, j, s: (j, 0)),
            pl.BlockSpec((tile, LANES), lambda b, j, s: (j, 0)),
        ],
        out_specs=pl.BlockSpec((None, tile, D_MODEL), lambda b, j, s: (b, j, 0)),
        scratch_shapes=[
            pltpu.VMEM((tile, ATTN_WIDTH), jnp.bfloat16),
            pltpu.VMEM((2 * N_KV_HEADS, BLOCK + tile, LANES), jnp.bfloat16),
            pltpu.VMEM((N_KV_HEADS, BLOCK + tile, 2 * LANES), jnp.bfloat16),
            pltpu.VMEM((CONV_PAD + tile, CONV_WIDTH), jnp.float32),
            pltpu.VMEM((tile, D_MODEL), jnp.bfloat16),
        ],
    )
    return pl.pallas_call(
        _layer_kernel,
        grid_spec=grid_spec,
        out_shape=jax.ShapeDtypeStruct(x.shape, x.dtype),
        compiler_params=pltpu.CompilerParams(
            dimension_semantics=("arbitrary", "arbitrary"),
            vmem_limit_bytes=VMEM_LIMIT_BYTES),
        name="hybrid_layer",
    )(sinks.astype(jnp.float32), x, norm_g.reshape(1, D_MODEL), final_g.reshape(1, D_MODEL),
      w_in.astype(jnp.bfloat16), w_out.astype(jnp.bfloat16), conv_w,
      cos_t, sin_lo, sin_hi)
```

```python
import jax
import jax.numpy as jnp
from jax import lax
from jax.experimental import pallas as pl
from jax.experimental.pallas import tpu as pltpu

D_MODEL = 1024
HEAD_DIM = 64
N_Q_HEADS = 8
N_KV_HEADS = 2
GROUP = N_Q_HEADS // N_KV_HEADS
ATTN_WIDTH = N_Q_HEADS * HEAD_DIM
KV_WIDTH = N_KV_HEADS * HEAD_DIM
BLOCK = 128
ROT_DIM = HEAD_DIM // 4
ROPE_THETA = 500000.0
CONV_WIDTH = D_MODEL - ATTN_WIDTH
CONV_K = 3
EPS = 1e-5

LANES = 128
BF16_ROWS = 16
CONV_PAD = 8
SEQ_TILE = 1024
SUB_TILE = 512
VMEM_LIMIT_BYTES = 56 * 1024 * 1024

OFF_Q = 0
OFF_KV = OFF_Q + ATTN_WIDTH
OFF_GA = OFF_KV + 2 * KV_WIDTH
OFF_B = OFF_GA + ATTN_WIDTH
OFF_C = OFF_B + CONV_WIDTH
OFF_H = OFF_C + CONV_WIDTH
OFF_GC = OFF_H + CONV_WIDTH


def _rope_tables(seq):
    half = ROT_DIM // 2
    inv_freq = ROPE_THETA ** (-jnp.arange(0, ROT_DIM, 2, dtype=jnp.float32) / ROT_DIM)
    ang = jnp.arange(seq, dtype=jnp.int32).astype(jnp.float32)[:, None] * inv_freq[None, :]
    cos, sin = jnp.cos(ang), jnp.sin(ang)
    ones = jnp.ones((seq, HEAD_DIM - ROT_DIM), jnp.float32)
    zeros_h = jnp.zeros((seq, half), jnp.float32)
    zeros_t = jnp.zeros((seq, HEAD_DIM - ROT_DIM), jnp.float32)
    cos_t = jnp.concatenate([cos, cos, ones], axis=1)
    sin_lo = jnp.concatenate([-sin, zeros_h, zeros_t], axis=1)
    sin_hi = jnp.concatenate([zeros_h, sin, zeros_t], axis=1)
    rep = LANES // HEAD_DIM
    return tuple(jnp.tile(t, (1, rep)) for t in (cos_t, sin_lo, sin_hi))


def _rms_norm(x, g):
    ms = jnp.mean(x * x, axis=-1, keepdims=True)
    return x * lax.rsqrt(ms + EPS) * g


def _silu(x):
    return x * (1.0 / (1.0 + jnp.exp(-x)))


def _layer_kernel(sinks_ref, x_ref, ng_ref, fg_ref, win_ref, wout_ref, convw_ref,
                  cos_ref, sinlo_ref, sinhi_ref, out_ref,
                  q_buf, kz_buf, vx_buf, u_buf, mix_buf):
    tile = x_ref.shape[0]
    j = pl.program_id(1)
    f32, bf16 = jnp.float32, jnp.bfloat16

    @pl.when(j == 0)
    def _():
        kz_buf[:, 0:BLOCK, :] = jnp.zeros((2 * N_KV_HEADS, BLOCK, LANES), bf16)
        vx_buf[:, 0:BLOCK, :] = jnp.ones((N_KV_HEADS, BLOCK, 2 * LANES), bf16)
        u_buf[0:CONV_PAD, :] = jnp.zeros((CONV_PAD, CONV_WIDTH), f32)

    low_half = lax.broadcasted_iota(jnp.int32, (SUB_TILE, LANES), 1) < HEAD_DIM
    qi = lax.broadcasted_iota(jnp.int32, (BLOCK, BLOCK), 0)
    kj = lax.broadcasted_iota(jnp.int32, (BLOCK, BLOCK), 1)
    neg_inf = jnp.float32(-jnp.inf)
    bias_cur = jnp.where(kj <= qi, 0.0, neg_inf)
    bias_prev_mid = jnp.where(kj > qi, 0.0, neg_inf)
    bias_prev_first = jnp.where((kj > qi) & (j > 0), 0.0, neg_inf)
    sink_slot = kj == 0
    low_half_blk = kj < HEAD_DIM
    top_r = lax.broadcasted_iota(jnp.int32, (BF16_ROWS, 2 * LANES), 0)
    top_c = lax.broadcasted_iota(jnp.int32, (BF16_ROWS, 2 * LANES), 1)
    sink_value_mask = (top_r == 0) & ((top_c < HEAD_DIM) | (top_c >= 2 * LANES - HEAD_DIM))
    nt_dims = (((1,), (1,)), ((), ()))
    scale = HEAD_DIM ** -0.5

    def sub_tile(r0):
        sub = slice(r0, r0 + SUB_TILE)
        x = x_ref[sub, :]
        xn = _rms_norm(x, ng_ref[...]).astype(bf16)

        def proj(off, width):
            return jnp.dot(xn, win_ref[:, off:off + width], preferred_element_type=f32)

        cos_t, sin_lo, sin_hi = cos_ref[sub, :], sinlo_ref[sub, :], sinhi_ref[sub, :]

        def rope(t):
            return (t * cos_t + pltpu.roll(t, LANES - ROT_DIM // 2, 1) * sin_lo
                    + pltpu.roll(t, ROT_DIM // 2, 1) * sin_hi)

        q = proj(OFF_Q, ATTN_WIDTH)
        for c in range(ATTN_WIDTH // LANES):
            qc = rope(q[:, c * LANES:(c + 1) * LANES]) * scale
            q_buf[sub, c * LANES:(c + 1) * LANES] = qc.astype(bf16)

        kv = proj(OFF_KV, 2 * KV_WIDTH)
        new_rows = slice(BLOCK + r0, BLOCK + r0 + SUB_TILE)
        kr = rope(kv[:, 0:KV_WIDTH])
        kr_sw = pltpu.roll(kr, HEAD_DIM, 1)
        kz_buf[0, new_rows, :] = jnp.where(low_half, kr, 0.0).astype(bf16)
        kz_buf[1, new_rows, :] = jnp.where(low_half, 0.0, kr_sw).astype(bf16)
        kz_buf[2, new_rows, :] = jnp.where(low_half, kr_sw, 0.0).astype(bf16)
        kz_buf[3, new_rows, :] = jnp.where(low_half, 0.0, kr).astype(bf16)
        vc = kv[:, KV_WIDTH:2 * KV_WIDTH]
        vc_sw = pltpu.roll(vc, HEAD_DIM, 1)
        vx_buf[0, new_rows, 0:LANES] = jnp.where(low_half, vc, 1.0).astype(bf16)
        vx_buf[0, new_rows, LANES:2 * LANES] = jnp.where(low_half, 1.0, vc_sw).astype(bf16)
        vx_buf[1, new_rows, 0:LANES] = jnp.where(low_half, vc_sw, 1.0).astype(bf16)
        vx_buf[1, new_rows, LANES:2 * LANES] = jnp.where(low_half, 1.0, vc).astype(bf16)

        g_attn = _silu(proj(OFF_GA, ATTN_WIDTH))

        for n in range(SUB_TILE // BLOCK):
            rows = slice(r0 + n * BLOCK, r0 + (n + 1) * BLOCK)
            keys = slice(r0 + n * BLOCK, r0 + (n + 2) * BLOCK)
            loc = slice(n * BLOCK, (n + 1) * BLOCK)
            bias_prev = bias_prev_first if (r0 == 0 and n == 0) else bias_prev_mid
            for g in range(N_KV_HEADS):
                base = g * GROUP * HEAD_DIM
                q2 = jnp.concatenate(
                    [q_buf[rows, base:base + LANES], q_buf[rows, base + LANES:base + 2 * LANES]], axis=0)
                vx = vx_buf[g, keys, :]
                vx = jnp.concatenate(
                    [jnp.where(sink_value_mask, 0.0, vx[0:BF16_ROWS]).astype(bf16), vx[BF16_ROWS:]], axis=0)
                pv = []
                for half in range(2):
                    s = lax.dot_general(q2, kz_buf[2 * g + half, keys, :], nt_dims,
                                        preferred_element_type=f32)
                    p = []
                    for c in range(2):
                        sink = sinks_ref[g * GROUP + 2 * c + half]
                        sc = s[c * BLOCK:(c + 1) * BLOCK]
                        s_prev = jnp.where(sink_slot, sink, sc[:, 0:BLOCK] + bias_prev)
                        s_cur = sc[:, BLOCK:2 * BLOCK] + bias_cur
                        m = jnp.max(jnp.maximum(s_prev, s_cur), axis=-1, keepdims=True)
                        p.append(jnp.concatenate(
                            [jnp.exp(s_prev - m), jnp.exp(s_cur - m)], axis=1).astype(bf16))
                    pv.append(jnp.dot(jnp.concatenate(p, axis=0), vx, preferred_element_type=f32))
                for c in range(2):
                    blk = slice(c * BLOCK, (c + 1) * BLOCK)
                    num = jnp.where(low_half_blk, pv[0][blk, 0:LANES], pv[1][blk, LANES:2 * LANES])
                    den = jnp.where(low_half_blk, pv[0][blk, LANES:2 * LANES], pv[1][blk, 0:LANES])
                    cols = slice(base + c * LANES, base + (c + 1) * LANES)
                    mix_buf[rows, cols] = (num / den * g_attn[loc, cols]).astype(bf16)

        u = proj(OFF_C, CONV_WIDTH) * proj(OFF_H, CONV_WIDTH)
        u0 = CONV_PAD + r0
        u_buf[u0:u0 + SUB_TILE, :] = u
        conv = (convw_ref[0:1, :] * u_buf[u0 - 2:u0 - 2 + SUB_TILE, :]
                + convw_ref[1:2, :] * u_buf[u0 - 1:u0 - 1 + SUB_TILE, :]
                + convw_ref[2:3, :] * u)
        y_conv = proj(OFF_B, CONV_WIDTH) * conv * _silu(proj(OFF_GC, CONV_WIDTH))
        mix_buf[sub, ATTN_WIDTH:ATTN_WIDTH + CONV_WIDTH] = y_conv.astype(bf16)

        h = x + jnp.dot(mix_buf[sub, :], wout_ref[...], preferred_element_type=f32)
        out_ref[sub, :] = _rms_norm(h, fg_ref[...])

    for r0 in range(0, tile, SUB_TILE):
        sub_tile(r0)

    u_buf[0:CONV_PAD, :] = u_buf[tile:tile + CONV_PAD, :]
    kz_buf[:, 0:BLOCK, :] = kz_buf[:, tile:tile + BLOCK, :]
    vx_buf[:, 0:BLOCK, :] = vx_buf[:, tile:tile + BLOCK, :]


@jax.jit
def kernel(x, norm_g, w_in, sinks, conv_w, w_out, final_g):
    batch, seq, d_model = x.shape
    tile = SEQ_TILE
    assert d_model == D_MODEL and seq % tile == 0 and tile % SUB_TILE == 0 and SUB_TILE % BLOCK == 0
    in_width = w_in.shape[1]
    cos_t, sin_lo, sin_hi = _rope_tables(seq)

    grid_spec = pltpu.PrefetchScalarGridSpec(
        num_scalar_prefetch=1,
        grid=(batch, seq // tile),
        in_specs=[
            pl.BlockSpec((None, tile, D_MODEL), lambda b, j, s: (b, j, 0)),
            pl.BlockSpec((1, D_MODEL), lambda b, j, s: (0, 0)),
            pl.BlockSpec((1, D_MODEL), lambda b, j, s: (0, 0)),
            pl.BlockSpec((D_MODEL, in_width), lambda b, j, s: (0, 0)),
            pl.BlockSpec((D_MODEL, D_MODEL), lambda b, j, s: (0, 0)),
            pl.BlockSpec((CONV_K, CONV_WIDTH), lambda b, j, s: (0, 0)),
            pl.BlockSpec((tile, LANES), lambda b, j, s: (j, 0)),
            pl.BlockSpec((tile, LANES), lambda b, j, s: (j, 0)),
            pl.BlockSpec((tile, LANES), lambda b, j, s: (j, 0)),
        ],
        out_specs=pl.BlockSpec((None, tile, D_MODEL), lambda b, j, s: (b, j, 0)),
        scratch_shapes=[
            pltpu.VMEM((tile, ATTN_WIDTH), jnp.bfloat16),
            pltpu.VMEM((2 * N_KV_HEADS, BLOCK + tile, LANES), jnp.bfloat16),
            pltpu.VMEM((N_KV_HEADS, BLOCK + tile, 2 * LANES), jnp.bfloat16),
            pltpu.VMEM((CONV_PAD + tile, CONV_WIDTH), jnp.float32),
            pltpu.VMEM((tile, D_MODEL), jnp.bfloat16),
        ],
    )
    return pl.pallas_call(
        _layer_kernel,
        grid_spec=grid_spec,
        out_shape=jax.ShapeDtypeStruct(x.shape, x.dtype),
        compiler_params=pltpu.CompilerParams(
            dimension_semantics=("arbitrary", "arbitrary"),
            vmem_limit_bytes=VMEM_LIMIT_BYTES),
        name="hybrid_layer",
    )(sinks.astype(jnp.float32), x, norm_g.reshape(1, D_MODEL), final_g.reshape(1, D_MODEL),
      w_in.astype(jnp.bfloat16), w_out.astype(jnp.bfloat16), conv_w,
      cos_t, sin_lo, sin_hi)
```

```python
import jax
import jax.numpy as jnp
from jax import lax
from jax.experimental import pallas as pl
from jax.experimental.pallas import tpu as pltpu

D_MODEL = 1024
HEAD_DIM = 64
N_Q_HEADS = 8
N_KV_HEADS = 2
GROUP = N_Q_HEADS // N_KV_HEADS
ATTN_WIDTH = N_Q_HEADS * HEAD_DIM
KV_WIDTH = N_KV_HEADS * HEAD_DIM
BLOCK = 128
ROT_DIM = HEAD_DIM // 4
ROPE_THETA = 500000.0
CONV_WIDTH = D_MODEL - ATTN_WIDTH
CONV_K = 3
EPS = 1e-5
LOG2_E = 1.4426950408889634

LANES = 128
BF16_ROWS = 16
CONV_PAD = 8
SEQ_TILE = 1024
SUB_TILE = 512
VMEM_LIMIT_BYTES = 56 * 1024 * 1024

OFF_Q = 0
OFF_KV = OFF_Q + ATTN_WIDTH
OFF_GA = OFF_KV + 2 * KV_WIDTH
OFF_B = OFF_GA + ATTN_WIDTH
OFF_C = OFF_B + CONV_WIDTH
OFF_H = OFF_C + CONV_WIDTH
OFF_GC = OFF_H + CONV_WIDTH


def _rope_tables(seq):
    half = ROT_DIM // 2
    inv_freq = ROPE_THETA ** (-jnp.arange(0, ROT_DIM, 2, dtype=jnp.float32) / ROT_DIM)
    ang = jnp.arange(seq, dtype=jnp.int32).astype(jnp.float32)[:, None] * inv_freq[None, :]
    cos, sin = jnp.cos(ang), jnp.sin(ang)
    ones = jnp.ones((seq, HEAD_DIM - ROT_DIM), jnp.float32)
    zeros_h = jnp.zeros((seq, half), jnp.float32)
    zeros_t = jnp.zeros((seq, HEAD_DIM - ROT_DIM), jnp.float32)
    cos_t = jnp.concatenate([cos, cos, ones], axis=1)
    sin_lo = jnp.concatenate([-sin, zeros_h, zeros_t], axis=1)
    sin_hi = jnp.concatenate([zeros_h, sin, zeros_t], axis=1)
    rep = LANES // HEAD_DIM
    return tuple(jnp.tile(t, (1, rep)) for t in (cos_t, sin_lo, sin_hi))


def _rms_norm(x, g):
    ms = jnp.mean(x * x, axis=-1, keepdims=True)
    return x * lax.rsqrt(ms + EPS) * g


def _silu(x):
    h = 0.5 * x
    return h + h * jnp.tanh(h)


def _layer_kernel(sinks_ref, x_ref, ng_ref, fg_ref, win_ref, wout_ref, convw_ref,
                  cos_ref, sinlo_ref, sinhi_ref, out_ref,
                  q_buf, kz_buf, vx_buf, u_buf, mix_buf):
    tile = x_ref.shape[0]
    j = pl.program_id(1)
    f32, bf16 = jnp.float32, jnp.bfloat16

    @pl.when(j == 0)
    def _():
        kz_buf[:, 0:BLOCK, :] = jnp.zeros((2 * N_KV_HEADS, BLOCK, LANES), bf16)
        vx_buf[:, 0:BLOCK, :] = jnp.ones((N_KV_HEADS, BLOCK, 2 * LANES), bf16)
        u_buf[0:CONV_PAD, :] = jnp.zeros((CONV_PAD, CONV_WIDTH), f32)

    low_half = lax.broadcasted_iota(jnp.int32, (SUB_TILE, LANES), 1) < HEAD_DIM
    qi = lax.broadcasted_iota(jnp.int32, (BLOCK, BLOCK), 0)
    kj = lax.broadcasted_iota(jnp.int32, (BLOCK, BLOCK), 1)
    neg_inf = jnp.float32(-jnp.inf)
    bias_cur = jnp.where(kj <= qi, 0.0, neg_inf)
    bias_prev_mid = jnp.where(kj > qi, 0.0, neg_inf)
    bias_prev_first = jnp.where((kj > qi) & (j > 0), 0.0, neg_inf)
    sink_slot = kj == 0
    bias_prev_sink = {}

    def prev_bias(head, first):
        if (head, first) not in bias_prev_sink:
            bias_prev_sink[head, first] = jnp.where(
                sink_slot, sinks_ref[head] * LOG2_E, bias_prev_first if first else bias_prev_mid)
        return bias_prev_sink[head, first]

    low_half_blk = kj < HEAD_DIM
    top_r = lax.broadcasted_iota(jnp.int32, (BF16_ROWS, 2 * LANES), 0)
    top_c = lax.broadcasted_iota(jnp.int32, (BF16_ROWS, 2 * LANES), 1)
    sink_value_mask = (top_r == 0) & ((top_c < HEAD_DIM) | (top_c >= 2 * LANES - HEAD_DIM))
    sink_key_mask = lax.broadcasted_iota(jnp.int32, (BF16_ROWS, LANES), 0) == 0
    nt_dims = (((1,), (1,)), ((), ()))
    scale = HEAD_DIM ** -0.5 * LOG2_E

    def sub_tile(r0):
        sub = slice(r0, r0 + SUB_TILE)
        x = x_ref[sub, :]
        xn = _rms_norm(x, ng_ref[...]).astype(bf16)

        def proj(off, width):
            return jnp.dot(xn, win_ref[:, off:off + width], preferred_element_type=f32)

        cos_t, sin_lo, sin_hi = cos_ref[sub, :], sinlo_ref[sub, :], sinhi_ref[sub, :]

        def rope(t):
            return (t * cos_t + pltpu.roll(t, LANES - ROT_DIM // 2, 1) * sin_lo
                    + pltpu.roll(t, ROT_DIM // 2, 1) * sin_hi)

        q = proj(OFF_Q, ATTN_WIDTH)
        for c in range(ATTN_WIDTH // LANES):
            qc = rope(q[:, c * LANES:(c + 1) * LANES]) * scale
            q_buf[sub, c * LANES:(c + 1) * LANES] = qc.astype(bf16)
        yield "proj"

        kv = proj(OFF_KV, 2 * KV_WIDTH)
        new_rows = slice(BLOCK + r0, BLOCK + r0 + SUB_TILE)
        kr = rope(kv[:, 0:KV_WIDTH])
        kr_sw = pltpu.roll(kr, HEAD_DIM, 1)
        kz_buf[0, new_rows, :] = jnp.where(low_half, kr, 0.0).astype(bf16)
        kz_buf[1, new_rows, :] = jnp.where(low_half, 0.0, kr_sw).astype(bf16)
        kz_buf[2, new_rows, :] = jnp.where(low_half, kr_sw, 0.0).astype(bf16)
        kz_buf[3, new_rows, :] = jnp.where(low_half, 0.0, kr).astype(bf16)
        vc = kv[:, KV_WIDTH:2 * KV_WIDTH]
        vc_sw = pltpu.roll(vc, HEAD_DIM, 1)
        vx_buf[0, new_rows, 0:LANES] = jnp.where(low_half, vc, 1.0).astype(bf16)
        vx_buf[0, new_rows, LANES:2 * LANES] = jnp.where(low_half, 1.0, vc_sw).astype(bf16)
        vx_buf[1, new_rows, 0:LANES] = jnp.where(low_half, vc_sw, 1.0).astype(bf16)
        vx_buf[1, new_rows, LANES:2 * LANES] = jnp.where(low_half, 1.0, vc).astype(bf16)
        yield "proj"

        g_attn = _silu(proj(OFF_GA, ATTN_WIDTH))
        yield "proj"

        u = proj(OFF_C, CONV_WIDTH) * proj(OFF_H, CONV_WIDTH)
        u0 = CONV_PAD + r0
        u_buf[u0:u0 + SUB_TILE, :] = u
        conv = (convw_ref[0:1, :] * u_buf[u0 - 2:u0 - 2 + SUB_TILE, :]
                + convw_ref[1:2, :] * u_buf[u0 - 1:u0 - 1 + SUB_TILE, :]
                + convw_ref[2:3, :] * u)
        yield "proj"
        y_conv = proj(OFF_B, CONV_WIDTH) * conv
        yield "proj"
        y_conv = y_conv * _silu(proj(OFF_GC, CONV_WIDTH))
        mix_buf[sub, ATTN_WIDTH:ATTN_WIDTH + CONV_WIDTH] = y_conv.astype(bf16)
        yield "proj"

        for n in range(SUB_TILE // BLOCK):
            rows = slice(r0 + n * BLOCK, r0 + (n + 1) * BLOCK)
            keys = slice(r0 + n * BLOCK, r0 + (n + 2) * BLOCK)
            loc = slice(n * BLOCK, (n + 1) * BLOCK)
            first = r0 == 0 and n == 0
            for g in range(N_KV_HEADS):
                base = g * GROUP * HEAD_DIM
                q2 = jnp.concatenate(
                    [q_buf[rows, base:base + LANES], q_buf[rows, base + LANES:base + 2 * LANES]], axis=0)
                vx = vx_buf[g, keys, :]
                vx = jnp.concatenate(
                    [jnp.where(sink_value_mask, 0.0, vx[0:BF16_ROWS]).astype(bf16), vx[BF16_ROWS:]], axis=0)
                pv = []
                for half in range(2):
                    kz = kz_buf[2 * g + half, keys, :]
                    kz = jnp.concatenate(
                        [jnp.where(sink_key_mask, 0.0, kz[0:BF16_ROWS]).astype(bf16), kz[BF16_ROWS:]], axis=0)
                    s = lax.dot_general(q2, kz, nt_dims, preferred_element_type=f32)
                    p = []
                    for c in range(2):
                        sc = s[c * BLOCK:(c + 1) * BLOCK]
                        s_prev = sc[:, 0:BLOCK] + prev_bias(g * GROUP + 2 * c + half, first)
                        s_cur = sc[:, BLOCK:2 * BLOCK] + bias_cur
                        m = jnp.max(jnp.maximum(s_prev, s_cur), axis=-1, keepdims=True)
                        p.append(jnp.concatenate(
                            [jnp.exp2(s_prev - m), jnp.exp2(s_cur - m)], axis=1).astype(bf16))
                    pv.append(jnp.dot(jnp.concatenate(p, axis=0), vx, preferred_element_type=f32))
                for c in range(2):
                    blk = slice(c * BLOCK, (c + 1) * BLOCK)
                    num = jnp.where(low_half_blk, pv[0][blk, 0:LANES], pv[1][blk, LANES:2 * LANES])
                    den = jnp.where(low_half_blk, pv[0][blk, LANES:2 * LANES], pv[1][blk, 0:LANES])
                    cols = slice(base + c * LANES, base + (c + 1) * LANES)
                    mix_buf[rows, cols] = (num / den * g_attn[loc, cols]).astype(bf16)
                yield "attn"

        h = x + jnp.dot(mix_buf[sub, :], wout_ref[...], preferred_element_type=f32)
        out_ref[sub, :] = _rms_norm(h, fg_ref[...])
        yield "out"

    n_proj, n_attn = 6, (SUB_TILE // BLOCK) * N_KV_HEADS
    passes = [sub_tile(r0) for r0 in range(0, tile, SUB_TILE)]
    for _ in range(n_proj):
        next(passes[0])
    for k, cur in enumerate(passes):
        nxt = passes[k + 1] if k + 1 < len(passes) else None
        proj_left = n_proj if nxt is not None else 0
        for _ in range(n_attn):
            next(cur)
            if proj_left:
                next(nxt)
                proj_left -= 1
        for _ in range(proj_left):
            next(nxt)
        next(cur)

    u_buf[0:CONV_PAD, :] = u_buf[tile:tile + CONV_PAD, :]
    kz_buf[:, 0:BLOCK, :] = kz_buf[:, tile:tile + BLOCK, :]
    vx_buf[:, 0:BLOCK, :] = vx_buf[:, tile:tile + BLOCK, :]


@jax.jit
def kernel(x, norm_g, w_in, sinks, conv_w, w_out, final_g):
    batch, seq, d_model = x.shape
    tile = SEQ_TILE
    assert d_model == D_MODEL and seq % tile == 0 and tile % SUB_TILE == 0 and SUB_TILE % BLOCK == 0
    in_width = w_in.shape[1]
    cos_t, sin_lo, sin_hi = _rope_tables(seq)

    grid_spec = pltpu.PrefetchScalarGridSpec(
        num_scalar_prefetch=1,
        grid=(batch, seq // tile),
        in_specs=[
            pl.BlockSpec((None, tile, D_MODEL), lambda b, j, s: (b, j, 0)),
            pl.BlockSpec((1, D_MODEL), lambda b, j, s: (0, 0)),
            pl.BlockSpec((1, D_MODEL), lambda b, j, s: (0, 0)),
            pl.BlockSpec((D_MODEL, in_width), lambda b, j, s: (0, 0)),
            pl.BlockSpec((D_MODEL, D_MODEL), lambda b, j, s: (0, 0)),
            pl.BlockSpec((CONV_K, CONV_WIDTH), lambda b, j, s: (0, 0)),
            pl.BlockSpec((tile, LANES), lambda b, j, s: (j, 0)),
            pl.BlockSpec((tile, LANES), lambda b, j, s: (j, 0)),
            pl.BlockSpec((tile, LANES), lambda b, j, s: (j, 0)),
        ],
        out_specs=pl.BlockSpec((None, tile, D_MODEL), lambda b, j, s: (b, j, 0)),
        scratch_shapes=[
            pltpu.VMEM((tile, ATTN_WIDTH), jnp.bfloat16),
            pltpu.VMEM((2 * N_KV_HEADS, BLOCK + tile, LANES), jnp.bfloat16),
            pltpu.VMEM((N_KV_HEADS, BLOCK + tile, 2 * LANES), jnp.bfloat16),
            pltpu.VMEM((CONV_PAD + tile, CONV_WIDTH), jnp.float32),
            pltpu.VMEM((tile, D_MODEL), jnp.bfloat16),
        ],
    )
    return pl.pallas_call(
        _layer_kernel,
        grid_spec=grid_spec,
        out_shape=jax.ShapeDtypeStruct(x.shape, x.dtype),
        compiler_params=pltpu.CompilerParams(
            dimension_semantics=("arbitrary", "arbitrary"),
            vmem_limit_bytes=VMEM_LIMIT_BYTES),
        name="hybrid_layer",
    )(sinks.astype(jnp.float32), x, norm_g.reshape(1, D_MODEL), final_g.reshape(1, D_MODEL),
      w_in.astype(jnp.bfloat16), w_out.astype(jnp.bfloat16), conv_w,
      cos_t, sin_lo, sin_hi)
```

```python
import functools

import jax
import jax.numpy as jnp
import numpy as np
from jax import lax
from jax.experimental import pallas as pl
from jax.experimental.pallas import tpu as pltpu

D_MODEL = 1024
HEAD_DIM = 64
N_Q_HEADS = 8
N_KV_HEADS = 2
GROUP = N_Q_HEADS // N_KV_HEADS
ATTN_WIDTH = N_Q_HEADS * HEAD_DIM
KV_WIDTH = N_KV_HEADS * HEAD_DIM
BLOCK = 128
ROT_DIM = HEAD_DIM // 4
ROPE_THETA = 500000.0
CONV_WIDTH = D_MODEL - ATTN_WIDTH
CONV_K = 3
EPS = 1e-5
LOG2_E = 1.4426950408889634

LANES = 128
BF16_ROWS = 16
CONV_PAD = 8
SEQ_TILE = 1024
SUB_TILE = 512
VMEM_LIMIT_BYTES = 56 * 1024 * 1024

OFF_Q = 0
OFF_KV = OFF_Q + ATTN_WIDTH
OFF_GA = OFF_KV + 2 * KV_WIDTH
OFF_B = OFF_GA + ATTN_WIDTH
OFF_C = OFF_B + CONV_WIDTH
OFF_H = OFF_C + CONV_WIDTH
OFF_GC = OFF_H + CONV_WIDTH


@functools.lru_cache(maxsize=None)
def _rope_tables(seq):
    half = ROT_DIM // 2
    inv_freq = ROPE_THETA ** (-np.arange(0, ROT_DIM, 2, dtype=np.float64) / ROT_DIM)
    ang = np.arange(seq, dtype=np.float64)[:, None] * inv_freq[None, :]
    cos, sin = np.cos(ang), np.sin(ang)
    ones = np.ones((seq, HEAD_DIM - ROT_DIM))
    zeros_h = np.zeros((seq, half))
    zeros_t = np.zeros((seq, HEAD_DIM - ROT_DIM))
    cos_t = np.concatenate([cos, cos, ones], axis=1)
    sin_lo = np.concatenate([-sin, zeros_h, zeros_t], axis=1)
    sin_hi = np.concatenate([zeros_h, sin, zeros_t], axis=1)
    rep = LANES // HEAD_DIM
    return tuple(np.tile(t, (1, rep)).astype(np.float32) for t in (cos_t, sin_lo, sin_hi))


def _rms_norm(x, g):
    ms = jnp.mean(x * x, axis=-1, keepdims=True)
    return x * lax.rsqrt(ms + EPS) * g


def _silu(x):
    h = 0.5 * x
    return h + h * jnp.tanh(h)


def _layer_kernel(sinks_ref, x_ref, ng_ref, fg_ref, win_ref, wout_ref, convw_ref,
                  cos_ref, sinlo_ref, sinhi_ref, out_ref,
                  q_buf, kz_buf, vx_buf, u_buf, mix_buf):
    tile = x_ref.shape[0]
    j = pl.program_id(1)
    f32, bf16 = jnp.float32, jnp.bfloat16

    @pl.when(j == 0)
    def _():
        kz_buf[:, 0:BLOCK, :] = jnp.zeros((N_KV_HEADS, BLOCK, LANES), bf16)
        vx_buf[:, 0:BLOCK, :] = jnp.ones((N_KV_HEADS, BLOCK, 2 * LANES), bf16)
        u_buf[0:CONV_PAD, :] = jnp.zeros((CONV_PAD, CONV_WIDTH), f32)

    low_half = lax.broadcasted_iota(jnp.int32, (SUB_TILE, LANES), 1) < HEAD_DIM
    qi = lax.broadcasted_iota(jnp.int32, (BLOCK, BLOCK), 0)
    kj = lax.broadcasted_iota(jnp.int32, (BLOCK, BLOCK), 1)
    neg_inf = jnp.float32(-jnp.inf)
    bias_cur = jnp.where(kj <= qi, 0.0, neg_inf)
    bias_prev_mid = jnp.where(kj > qi, 0.0, neg_inf)
    bias_prev_first = jnp.where((kj > qi) & (j > 0), 0.0, neg_inf)
    sink_slot = kj == 0
    bias_prev_sink = {}

    def prev_bias(head, first):
        if (head, first) not in bias_prev_sink:
            bias_prev_sink[head, first] = jnp.where(
                sink_slot, sinks_ref[head] * LOG2_E, bias_prev_first if first else bias_prev_mid)
        return bias_prev_sink[head, first]

    low_half_blk = kj < HEAD_DIM
    top_r = lax.broadcasted_iota(jnp.int32, (BF16_ROWS, 2 * LANES), 0)
    top_c = lax.broadcasted_iota(jnp.int32, (BF16_ROWS, 2 * LANES), 1)
    sink_value_mask = (top_r == 0) & ((top_c < HEAD_DIM) | (top_c >= 2 * LANES - HEAD_DIM))
    sink_key_mask = lax.broadcasted_iota(jnp.int32, (BF16_ROWS, LANES), 0) == 0
    nt_dims = (((1,), (1,)), ((), ()))
    scale = HEAD_DIM ** -0.5 * LOG2_E

    def sub_tile(r0):
        sub = slice(r0, r0 + SUB_TILE)
        x = x_ref[sub, :]
        xn = _rms_norm(x, ng_ref[...]).astype(bf16)

        def proj(off, width):
            return jnp.dot(xn, win_ref[:, off:off + width], preferred_element_type=f32)

        cos_t, sin_lo, sin_hi = cos_ref[sub, :], sinlo_ref[sub, :], sinhi_ref[sub, :]

        def rope(t):
            return (t * cos_t + pltpu.roll(t, LANES - ROT_DIM // 2, 1) * sin_lo
                    + pltpu.roll(t, ROT_DIM // 2, 1) * sin_hi)

        q = proj(OFF_Q, ATTN_WIDTH)
        for c in range(ATTN_WIDTH // LANES):
            qc = rope(q[:, c * LANES:(c + 1) * LANES]) * scale
            q_buf[sub, c * LANES:(c + 1) * LANES] = qc.astype(bf16)
            q_buf[sub, ATTN_WIDTH + c * LANES:ATTN_WIDTH + (c + 1) * LANES] = (
                pltpu.roll(qc, HEAD_DIM, 1).astype(bf16))
        yield "proj"

        kv = proj(OFF_KV, 2 * KV_WIDTH)
        new_rows = slice(BLOCK + r0, BLOCK + r0 + SUB_TILE)
        kr = rope(kv[:, 0:KV_WIDTH])
        kr_sw = pltpu.roll(kr, HEAD_DIM, 1)
        kz_buf[0, new_rows, :] = jnp.where(low_half, kr, 0.0).astype(bf16)
        kz_buf[1, new_rows, :] = jnp.where(low_half, kr_sw, 0.0).astype(bf16)
        vc = kv[:, KV_WIDTH:2 * KV_WIDTH]
        vc_sw = pltpu.roll(vc, HEAD_DIM, 1)
        vx_buf[0, new_rows, 0:LANES] = jnp.where(low_half, vc, 1.0).astype(bf16)
        vx_buf[0, new_rows, LANES:2 * LANES] = jnp.where(low_half, 1.0, vc_sw).astype(bf16)
        vx_buf[1, new_rows, 0:LANES] = jnp.where(low_half, vc_sw, 1.0).astype(bf16)
        vx_buf[1, new_rows, LANES:2 * LANES] = jnp.where(low_half, 1.0, vc).astype(bf16)
        yield "proj"

        g_attn = _silu(proj(OFF_GA, ATTN_WIDTH))
        yield "proj"

        u = proj(OFF_C, CONV_WIDTH) * proj(OFF_H, CONV_WIDTH)
        u0 = CONV_PAD + r0
        u_buf[u0:u0 + SUB_TILE, :] = u
        conv = (convw_ref[0:1, :] * u_buf[u0 - 2:u0 - 2 + SUB_TILE, :]
                + convw_ref[1:2, :] * u_buf[u0 - 1:u0 - 1 + SUB_TILE, :]
                + convw_ref[2:3, :] * u)
        yield "proj"
        y_conv = proj(OFF_B, CONV_WIDTH) * conv
        yield "proj"
        y_conv = y_conv * _silu(proj(OFF_GC, CONV_WIDTH))
        mix_buf[sub, ATTN_WIDTH:ATTN_WIDTH + CONV_WIDTH] = y_conv.astype(bf16)
        yield "proj"

        for n in range(SUB_TILE // BLOCK):
            rows = slice(r0 + n * BLOCK, r0 + (n + 1) * BLOCK)
            keys = slice(r0 + n * BLOCK, r0 + (n + 2) * BLOCK)
            loc = slice(n * BLOCK, (n + 1) * BLOCK)
            first = r0 == 0 and n == 0
            for g in range(N_KV_HEADS):
                base = g * GROUP * HEAD_DIM
                q4 = jnp.concatenate(
                    [q_buf[rows, off:off + LANES]
                     for off in (base, base + LANES, ATTN_WIDTH + base, ATTN_WIDTH + base + LANES)], axis=0)
                vx = vx_buf[g, keys, :]
                vx = jnp.concatenate(
                    [jnp.where(sink_value_mask, 0.0, vx[0:BF16_ROWS]).astype(bf16), vx[BF16_ROWS:]], axis=0)
                kz = kz_buf[g, keys, :]
                kz = jnp.concatenate(
                    [jnp.where(sink_key_mask, 0.0, kz[0:BF16_ROWS]).astype(bf16), kz[BF16_ROWS:]], axis=0)
                s = lax.dot_general(q4, kz, nt_dims, preferred_element_type=f32)
                p = []
                for slab, head in enumerate((0, 2, 1, 3)):
                    sc = s[slab * BLOCK:(slab + 1) * BLOCK]
                    s_prev = sc[:, 0:BLOCK] + prev_bias(g * GROUP + head, first)
                    s_cur = sc[:, BLOCK:2 * BLOCK] + bias_cur
                    m = jnp.max(jnp.maximum(s_prev, s_cur), axis=-1, keepdims=True)
                    p.append(jnp.concatenate(
                        [jnp.exp2(s_prev - m), jnp.exp2(s_cur - m)], axis=1).astype(bf16))
                pv = jnp.dot(jnp.concatenate(p, axis=0), vx, preferred_element_type=f32)
                for c in range(2):
                    lo = pv[c * BLOCK:(c + 1) * BLOCK]
                    hi = pv[(2 + c) * BLOCK:(3 + c) * BLOCK]
                    num = jnp.where(low_half_blk, lo[:, 0:LANES], hi[:, LANES:2 * LANES])
                    den = jnp.where(low_half_blk, lo[:, LANES:2 * LANES], hi[:, 0:LANES])
                    cols = slice(base + c * LANES, base + (c + 1) * LANES)
                    mix_buf[rows, cols] = (num / den * g_attn[loc, cols]).astype(bf16)
                yield "attn"

        h = x + jnp.dot(mix_buf[sub, :], wout_ref[...], preferred_element_type=f32)
        out_ref[sub, :] = _rms_norm(h, fg_ref[...])
        yield "out"

    n_proj, n_attn = 6, (SUB_TILE // BLOCK) * N_KV_HEADS
    passes = [sub_tile(r0) for r0 in range(0, tile, SUB_TILE)]
    for _ in range(n_proj):
        next(passes[0])
    for k, cur in enumerate(passes):
        nxt = passes[k + 1] if k + 1 < len(passes) else None
        proj_left = n_proj if nxt is not None else 0
        for _ in range(n_attn):
            next(cur)
            if proj_left:
                next(nxt)
                proj_left -= 1
        for _ in range(proj_left):
            next(nxt)
        next(cur)

    u_buf[0:CONV_PAD, :] = u_buf[tile:tile + CONV_PAD, :]
    kz_buf[:, 0:BLOCK, :] = kz_buf[:, tile:tile + BLOCK, :]
    vx_buf[:, 0:BLOCK, :] = vx_buf[:, tile:tile + BLOCK, :]


@jax.jit
def kernel(x, norm_g, w_in, sinks, conv_w, w_out, final_g):
    batch, seq, d_model = x.shape
    tile = SEQ_TILE
    assert d_model == D_MODEL and seq % tile == 0 and tile % SUB_TILE == 0 and SUB_TILE % BLOCK == 0
    in_width = w_in.shape[1]
    cos_t, sin_lo, sin_hi = _rope_tables(seq)

    grid_spec = pltpu.PrefetchScalarGridSpec(
        num_scalar_prefetch=1,
        grid=(batch, seq // tile),
        in_specs=[
            pl.BlockSpec((None, tile, D_MODEL), lambda b, j, s: (b, j, 0)),
            pl.BlockSpec((1, D_MODEL), lambda b, j, s: (0, 0)),
            pl.BlockSpec((1, D_MODEL), lambda b, j, s: (0, 0)),
            pl.BlockSpec((D_MODEL, in_width), lambda b, j, s: (0, 0)),
            pl.BlockSpec((D_MODEL, D_MODEL), lambda b, j, s: (0, 0)),
            pl.BlockSpec((CONV_K, CONV_WIDTH), lambda b, j, s: (0, 0)),
            pl.BlockSpec((tile, LANES), lambda b, j, s: (j, 0)),
            pl.BlockSpec((tile, LANES), lambda b, j, s: (j, 0)),
            pl.BlockSpec((tile, LANES), lambda b, j, s: (j, 0)),
        ],
        out_specs=pl.BlockSpec((None, tile, D_MODEL), lambda b, j, s: (b, j, 0)),
        scratch_shapes=[
            pltpu.VMEM((tile, 2 * ATTN_WIDTH), jnp.bfloat16),
            pltpu.VMEM((N_KV_HEADS, BLOCK + tile, LANES), jnp.bfloat16),
            pltpu.VMEM((N_KV_HEADS, BLOCK + tile, 2 * LANES), jnp.bfloat16),
            pltpu.VMEM((CONV_PAD + tile, CONV_WIDTH), jnp.float32),
            pltpu.VMEM((tile, D_MODEL), jnp.bfloat16),
        ],
    )
    return pl.pallas_call(
        _layer_kernel,
        grid_spec=grid_spec,
        out_shape=jax.ShapeDtypeStruct(x.shape, x.dtype),
        compiler_params=pltpu.CompilerParams(
            dimension_semantics=("arbitrary", "arbitrary"),
            vmem_limit_bytes=VMEM_LIMIT_BYTES),
        name="hybrid_layer",
    )(sinks.astype(jnp.float32), x, norm_g.reshape(1, D_MODEL), final_g.reshape(1, D_MODEL),
      w_in.astype(jnp.bfloat16), w_out.astype(jnp.bfloat16), conv_w,
      cos_t, sin_lo, sin_hi)
```

```python
import functools

import jax
import jax.numpy as jnp
import numpy as np
from jax import lax
from jax.experimental import pallas as pl
from jax.experimental.pallas import tpu as pltpu

D_MODEL = 1024
HEAD_DIM = 64
N_Q_HEADS = 8
N_KV_HEADS = 2
GROUP = N_Q_HEADS // N_KV_HEADS
ATTN_WIDTH = N_Q_HEADS * HEAD_DIM
KV_WIDTH = N_KV_HEADS * HEAD_DIM
BLOCK = 128
ROT_DIM = HEAD_DIM // 4
ROPE_THETA = 500000.0
CONV_WIDTH = D_MODEL - ATTN_WIDTH
CONV_K = 3
EPS = 1e-5
LOG2_E = 1.4426950408889634

LANES = 128
BF16_ROWS = 16
CONV_PAD = 8
SEQ_TILE = 1024
SUB_TILE = 512
VMEM_LIMIT_BYTES = 56 * 1024 * 1024
TRACE_ORDER = ("Aq Aq Aq Ac Ac Ac "
               "Aq Bq Aq Bq Aq Bq Aq Bc Aq Bc Aq Bc Aq Aq "
               "Bq Bq Ao Bq Bq Bq Bq Bq Bq Bo")

OFF_Q = 0
OFF_KV = OFF_Q + ATTN_WIDTH
OFF_GA = OFF_KV + 2 * KV_WIDTH
OFF_B = OFF_GA + ATTN_WIDTH
OFF_C = OFF_B + CONV_WIDTH
OFF_H = OFF_C + CONV_WIDTH
OFF_GC = OFF_H + CONV_WIDTH


@functools.lru_cache(maxsize=None)
def _rope_tables(seq):
    half = ROT_DIM // 2
    inv_freq = ROPE_THETA ** (-np.arange(0, ROT_DIM, 2, dtype=np.float64) / ROT_DIM)
    ang = np.arange(seq, dtype=np.float64)[:, None] * inv_freq[None, :]
    cos, sin = np.cos(ang), np.sin(ang)
    ones = np.ones((seq, HEAD_DIM - ROT_DIM))
    zeros_h = np.zeros((seq, half))
    zeros_t = np.zeros((seq, HEAD_DIM - ROT_DIM))
    cos_t = np.concatenate([cos, cos, ones], axis=1)
    sin_lo = np.concatenate([-sin, zeros_h, zeros_t], axis=1)
    sin_hi = np.concatenate([zeros_h, sin, zeros_t], axis=1)
    rep = LANES // HEAD_DIM
    return tuple(np.tile(t, (1, rep)).astype(np.float32) for t in (cos_t, sin_lo, sin_hi))


def _rms_norm(x, g):
    ms = jnp.mean(x * x, axis=-1, keepdims=True)
    return x * lax.rsqrt(ms + EPS) * g


def _silu(x):
    h = 0.5 * x
    return h + h * jnp.tanh(h)


def _layer_kernel(sinks_ref, x_ref, ng_ref, fg_ref, win_ref, wout_ref, convw_ref,
                  cos_ref, sinlo_ref, sinhi_ref, out_ref,
                  q_buf, kz_buf, vx_buf, u_buf, mix_buf):
    tile = x_ref.shape[0]
    j = pl.program_id(1)
    f32, bf16 = jnp.float32, jnp.bfloat16

    @pl.when(j == 0)
    def _():
        kz_buf[:, 0:BLOCK, :] = jnp.zeros((N_KV_HEADS, BLOCK, LANES), bf16)
        vx_buf[:, 0:BLOCK, :] = jnp.ones((N_KV_HEADS, BLOCK, 2 * LANES), bf16)
        u_buf[0:CONV_PAD, :] = jnp.zeros((CONV_PAD, CONV_WIDTH), f32)

    low_half = lax.broadcasted_iota(jnp.int32, (SUB_TILE, LANES), 1) < HEAD_DIM
    qi = lax.broadcasted_iota(jnp.int32, (BLOCK, BLOCK), 0)
    kj = lax.broadcasted_iota(jnp.int32, (BLOCK, BLOCK), 1)
    neg_inf = jnp.float32(-jnp.inf)
    bias_cur = jnp.where(kj <= qi, 0.0, neg_inf)
    bias_prev_mid = jnp.where(kj > qi, 0.0, neg_inf)
    bias_prev_first = jnp.where((kj > qi) & (j > 0), 0.0, neg_inf)
    sink_slot = kj == 0
    bias_prev_sink = {}

    def prev_bias(head, first):
        if (head, first) not in bias_prev_sink:
            bias_prev_sink[head, first] = jnp.where(
                sink_slot, sinks_ref[head] * LOG2_E, bias_prev_first if first else bias_prev_mid)
        return bias_prev_sink[head, first]

    low_half_blk = kj < HEAD_DIM
    top_r = lax.broadcasted_iota(jnp.int32, (BF16_ROWS, 2 * LANES), 0)
    top_c = lax.broadcasted_iota(jnp.int32, (BF16_ROWS, 2 * LANES), 1)
    sink_value_mask = (top_r == 0) & ((top_c < HEAD_DIM) | (top_c >= 2 * LANES - HEAD_DIM))
    sink_key_mask = lax.broadcasted_iota(jnp.int32, (BF16_ROWS, LANES), 0) == 0
    nt_dims = (((1,), (1,)), ((), ()))
    scale = HEAD_DIM ** -0.5 * LOG2_E

    def sub_tile(r0):
        sub = slice(r0, r0 + SUB_TILE)
        x = x_ref[sub, :]
        xn = _rms_norm(x, ng_ref[...]).astype(bf16)

        def proj(off, width):
            return jnp.dot(xn, win_ref[:, off:off + width], preferred_element_type=f32)

        def attention_side():
            cos_t, sin_lo, sin_hi = cos_ref[sub, :], sinlo_ref[sub, :], sinhi_ref[sub, :]

            def rope(t):
                return (t * cos_t + pltpu.roll(t, LANES - ROT_DIM // 2, 1) * sin_lo
                        + pltpu.roll(t, ROT_DIM // 2, 1) * sin_hi)

            q = proj(OFF_Q, ATTN_WIDTH)
            for c in range(ATTN_WIDTH // LANES):
                qc = rope(q[:, c * LANES:(c + 1) * LANES]) * scale
                q_buf[sub, c * LANES:(c + 1) * LANES] = qc.astype(bf16)
                q_buf[sub, ATTN_WIDTH + c * LANES:ATTN_WIDTH + (c + 1) * LANES] = (
                    pltpu.roll(qc, HEAD_DIM, 1).astype(bf16))
            yield

            kv = proj(OFF_KV, 2 * KV_WIDTH)
            new_rows = slice(BLOCK + r0, BLOCK + r0 + SUB_TILE)
            kr = rope(kv[:, 0:KV_WIDTH])
            kr_sw = pltpu.roll(kr, HEAD_DIM, 1)
            kz_buf[0, new_rows, :] = jnp.where(low_half, kr, 0.0).astype(bf16)
            kz_buf[1, new_rows, :] = jnp.where(low_half, kr_sw, 0.0).astype(bf16)
            vc = kv[:, KV_WIDTH:2 * KV_WIDTH]
            vc_sw = pltpu.roll(vc, HEAD_DIM, 1)
            vx_buf[0, new_rows, 0:LANES] = jnp.where(low_half, vc, 1.0).astype(bf16)
            vx_buf[0, new_rows, LANES:2 * LANES] = jnp.where(low_half, 1.0, vc_sw).astype(bf16)
            vx_buf[1, new_rows, 0:LANES] = jnp.where(low_half, vc_sw, 1.0).astype(bf16)
            vx_buf[1, new_rows, LANES:2 * LANES] = jnp.where(low_half, 1.0, vc).astype(bf16)
            yield

            g_attn = _silu(proj(OFF_GA, ATTN_WIDTH))
            yield

            for n in range(SUB_TILE // BLOCK):
                rows = slice(r0 + n * BLOCK, r0 + (n + 1) * BLOCK)
                keys = slice(r0 + n * BLOCK, r0 + (n + 2) * BLOCK)
                loc = slice(n * BLOCK, (n + 1) * BLOCK)
                first = r0 == 0 and n == 0
                for g in range(N_KV_HEADS):
                    base = g * GROUP * HEAD_DIM
                    q4 = jnp.concatenate(
                        [q_buf[rows, off:off + LANES]
                         for off in (base, base + LANES, ATTN_WIDTH + base, ATTN_WIDTH + base + LANES)], axis=0)
                    vx = vx_buf[g, keys, :]
                    vx = jnp.concatenate(
                        [jnp.where(sink_value_mask, 0.0, vx[0:BF16_ROWS]).astype(bf16), vx[BF16_ROWS:]], axis=0)
                    kz = kz_buf[g, keys, :]
                    kz = jnp.concatenate(
                        [jnp.where(sink_key_mask, 0.0, kz[0:BF16_ROWS]).astype(bf16), kz[BF16_ROWS:]], axis=0)
                    s = lax.dot_general(q4, kz, nt_dims, preferred_element_type=f32)
                    p = []
                    for slab, head in enumerate((0, 2, 1, 3)):
                        sc = s[slab * BLOCK:(slab + 1) * BLOCK]
                        s_prev = sc[:, 0:BLOCK] + prev_bias(g * GROUP + head, first)
                        s_cur = sc[:, BLOCK:2 * BLOCK] + bias_cur
                        m = jnp.max(jnp.maximum(s_prev, s_cur), axis=-1, keepdims=True)
                        p.append(jnp.concatenate(
                            [jnp.exp2(s_prev - m), jnp.exp2(s_cur - m)], axis=1).astype(bf16))
                    pv = jnp.dot(jnp.concatenate(p, axis=0), vx, preferred_element_type=f32)
                    for c in range(2):
                        lo = pv[c * BLOCK:(c + 1) * BLOCK]
                        hi = pv[(2 + c) * BLOCK:(3 + c) * BLOCK]
                        num = jnp.where(low_half_blk, lo[:, 0:LANES], hi[:, LANES:2 * LANES])
                        den = jnp.where(low_half_blk, lo[:, LANES:2 * LANES], hi[:, 0:LANES])
                        cols = slice(base + c * LANES, base + (c + 1) * LANES)
                        mix_buf[rows, cols] = (num / den * g_attn[loc, cols]).astype(bf16)
                    yield

        def conv_side():
            u = proj(OFF_C, CONV_WIDTH) * proj(OFF_H, CONV_WIDTH)
            u0 = CONV_PAD + r0
            u_buf[u0:u0 + SUB_TILE, :] = u
            conv = (convw_ref[0:1, :] * u_buf[u0 - 2:u0 - 2 + SUB_TILE, :]
                    + convw_ref[1:2, :] * u_buf[u0 - 1:u0 - 1 + SUB_TILE, :]
                    + convw_ref[2:3, :] * u)
            yield
            y_conv = proj(OFF_B, CONV_WIDTH) * conv
            yield
            y_conv = y_conv * _silu(proj(OFF_GC, CONV_WIDTH))
            mix_buf[sub, ATTN_WIDTH:ATTN_WIDTH + CONV_WIDTH] = y_conv.astype(bf16)
            yield

        def output():
            h = x + jnp.dot(mix_buf[sub, :], wout_ref[...], preferred_element_type=f32)
            out_ref[sub, :] = _rms_norm(h, fg_ref[...])
            yield

        return {"q": attention_side(), "c": conv_side(), "o": output()}

    assert tile == 2 * SUB_TILE and SUB_TILE == 4 * BLOCK
    streams = {"A": sub_tile(0), "B": sub_tile(SUB_TILE)}
    for token in TRACE_ORDER.split():
        next(streams[token[0]][token[1]])

    u_buf[0:CONV_PAD, :] = u_buf[tile:tile + CONV_PAD, :]
    kz_buf[:, 0:BLOCK, :] = kz_buf[:, tile:tile + BLOCK, :]
    vx_buf[:, 0:BLOCK, :] = vx_buf[:, tile:tile + BLOCK, :]


@jax.jit
def kernel(x, norm_g, w_in, sinks, conv_w, w_out, final_g):
    batch, seq, d_model = x.shape
    tile = SEQ_TILE
    assert d_model == D_MODEL and seq % tile == 0 and tile % SUB_TILE == 0 and SUB_TILE % BLOCK == 0
    in_width = w_in.shape[1]
    cos_t, sin_lo, sin_hi = _rope_tables(seq)

    grid_spec = pltpu.PrefetchScalarGridSpec(
        num_scalar_prefetch=1,
        grid=(batch, seq // tile),
        in_specs=[
            pl.BlockSpec((None, tile, D_MODEL), lambda b, j, s: (b, j, 0)),
            pl.BlockSpec((1, D_MODEL), lambda b, j, s: (0, 0)),
            pl.BlockSpec((1, D_MODEL), lambda b, j, s: (0, 0)),
            pl.BlockSpec((D_MODEL, in_width), lambda b, j, s: (0, 0)),
            pl.BlockSpec((D_MODEL, D_MODEL), lambda b, j, s: (0, 0)),
            pl.BlockSpec((CONV_K, CONV_WIDTH), lambda b, j, s: (0, 0)),
            pl.BlockSpec((tile, LANES), lambda b, j, s: (j, 0)),
            pl.BlockSpec((tile, LANES), lambda b, j, s: (j, 0)),
            pl.BlockSpec((tile, LANES), lambda b, j, s: (j, 0)),
        ],
        out_specs=pl.BlockSpec((None, tile, D_MODEL), lambda b, j, s: (b, j, 0)),
        scratch_shapes=[
            pltpu.VMEM((tile, 2 * ATTN_WIDTH), jnp.bfloat16),
            pltpu.VMEM((N_KV_HEADS, BLOCK + tile, LANES), jnp.bfloat16),
            pltpu.VMEM((N_KV_HEADS, BLOCK + tile, 2 * LANES), jnp.bfloat16),
            pltpu.VMEM((CONV_PAD + tile, CONV_WIDTH), jnp.float32),
            pltpu.VMEM((tile, D_MODEL), jnp.bfloat16),
        ],
    )
    return pl.pallas_call(
        _layer_kernel,
        grid_spec=grid_spec,
        out_shape=jax.ShapeDtypeStruct(x.shape, x.dtype),
        compiler_params=pltpu.CompilerParams(
            dimension_semantics=("arbitrary", "arbitrary"),
            vmem_limit_bytes=VMEM_LIMIT_BYTES),
        name="hybrid_layer",
    )(sinks.astype(jnp.float32), x, norm_g.reshape(1, D_MODEL), final_g.reshape(1, D_MODEL),
      w_in.astype(jnp.bfloat16), w_out.astype(jnp.bfloat16), conv_w,
      cos_t, sin_lo, sin_hi)
```

```python
import functools

import jax
import jax.numpy as jnp
import numpy as np
from jax import lax
from jax.experimental import pallas as pl
from jax.experimental.pallas import tpu as pltpu

D_MODEL = 1024
HEAD_DIM = 64
N_Q_HEADS = 8
N_KV_HEADS = 2
GROUP = N_Q_HEADS // N_KV_HEADS
ATTN_WIDTH = N_Q_HEADS * HEAD_DIM
KV_WIDTH = N_KV_HEADS * HEAD_DIM
BLOCK = 128
ROT_DIM = HEAD_DIM // 4
ROPE_THETA = 500000.0
CONV_WIDTH = D_MODEL - ATTN_WIDTH
CONV_K = 3
EPS = 1e-5
LOG2_E = 1.4426950408889634

LANES = 128
BF16_ROWS = 16
CONV_PAD = 8
SEQ_TILE = 1024
SUB_TILE = 512
VMEM_LIMIT_BYTES = 56 * 1024 * 1024
TRACE_ORDER = ("Aq Aq Aq Ac Ac Ac "
               "Aq Bq Aq Bq Aq Bq Aq Bc Aq Bc Aq Bc Aq Aq "
               "Bq Bq Ao Bq Bq Bq Bq Bq Bq Bo")

OFF_Q = 0
OFF_KV = OFF_Q + ATTN_WIDTH
OFF_GA = OFF_KV + 2 * KV_WIDTH
OFF_B = OFF_GA + ATTN_WIDTH
OFF_C = OFF_B + CONV_WIDTH
OFF_H = OFF_C + CONV_WIDTH
OFF_GC = OFF_H + CONV_WIDTH


def _split_head_columns(base):
    half, rest = ROT_DIM // 2, (HEAD_DIM - ROT_DIM) // 2
    order = []
    for part in range(2):
        for head in range(LANES // HEAD_DIM):
            h0 = base + head * HEAD_DIM
            order += list(range(h0 + part * half, h0 + (part + 1) * half))
            order += list(range(h0 + ROT_DIM + part * rest, h0 + ROT_DIM + (part + 1) * rest))
    return order


def _permute_qk_columns(w_in):
    cols = []
    for base in range(OFF_Q, OFF_KV + KV_WIDTH, LANES):
        cols += _split_head_columns(base)
    runs, start = [], 0
    for i in range(1, len(cols) + 1):
        if i == len(cols) or cols[i] != cols[i - 1] + 1:
            runs.append((cols[start], cols[i - 1] + 1))
            start = i
    return jnp.concatenate([w_in[:, lo:hi] for lo, hi in runs] + [w_in[:, OFF_KV + KV_WIDTH:]], axis=1)


@functools.lru_cache(maxsize=None)
def _rope_tables(seq):
    half, quarter = ROT_DIM // 2, HEAD_DIM // 2
    inv_freq = ROPE_THETA ** (-np.arange(0, ROT_DIM, 2, dtype=np.float64) / ROT_DIM)
    ang = np.arange(seq, dtype=np.float64)[:, None] * inv_freq[None, :]
    cos, sin = np.cos(ang), np.sin(ang)
    ones = np.ones((seq, quarter - half))
    zeros = np.zeros((seq, quarter - half))
    cos_q = np.concatenate([cos, ones], axis=1)
    sin_q = np.concatenate([sin, zeros], axis=1)
    cos_t = np.concatenate([cos_q] * 4, axis=1)
    sin_t = np.concatenate([-sin_q, -sin_q, sin_q, sin_q], axis=1)
    return cos_t.astype(np.float32), sin_t.astype(np.float32)


def _rms_norm(x, g):
    ms = jnp.mean(x * x, axis=-1, keepdims=True)
    return x * lax.rsqrt(ms + EPS) * g


def _silu(x):
    h = 0.5 * x
    return h + h * jnp.tanh(h)


def _layer_kernel(sinks_ref, x_ref, ng_ref, fg_ref, win_ref, wout_ref, convw_ref,
                  cos_ref, sin_ref, out_ref,
                  q_buf, kz_buf, vx_buf, u_buf, mix_buf):
    tile = x_ref.shape[0]
    j = pl.program_id(1)
    f32, bf16 = jnp.float32, jnp.bfloat16

    @pl.when(j == 0)
    def _():
        kz_buf[:, 0:BLOCK, :] = jnp.zeros((N_KV_HEADS, BLOCK, LANES), bf16)
        vx_buf[:, 0:BLOCK, :] = jnp.ones((N_KV_HEADS, BLOCK, 2 * LANES), bf16)
        u_buf[0:CONV_PAD, :] = jnp.zeros((CONV_PAD, CONV_WIDTH), f32)

    lane = lax.broadcasted_iota(jnp.int32, (SUB_TILE, LANES), 1)
    low_half = lane < HEAD_DIM
    first_head = lane % HEAD_DIM < HEAD_DIM // 2
    qi = lax.broadcasted_iota(jnp.int32, (BLOCK, BLOCK), 0)
    kj = lax.broadcasted_iota(jnp.int32, (BLOCK, BLOCK), 1)
    neg_inf = jnp.float32(-jnp.inf)
    bias_cur = jnp.where(kj <= qi, 0.0, neg_inf)
    bias_prev_mid = jnp.where(kj > qi, 0.0, neg_inf)
    bias_prev_first = jnp.where((kj > qi) & (j > 0), 0.0, neg_inf)
    sink_slot = kj == 0
    bias_prev_sink = {}

    def prev_bias(head, first):
        if (head, first) not in bias_prev_sink:
            bias_prev_sink[head, first] = jnp.where(
                sink_slot, sinks_ref[head] * LOG2_E, bias_prev_first if first else bias_prev_mid)
        return bias_prev_sink[head, first]

    low_half_blk = kj < HEAD_DIM
    top_r = lax.broadcasted_iota(jnp.int32, (BF16_ROWS, 2 * LANES), 0)
    top_c = lax.broadcasted_iota(jnp.int32, (BF16_ROWS, 2 * LANES), 1)
    sink_value_mask = (top_r == 0) & ((top_c < HEAD_DIM) | (top_c >= 2 * LANES - HEAD_DIM))
    sink_key_mask = lax.broadcasted_iota(jnp.int32, (BF16_ROWS, LANES), 0) == 0
    nt_dims = (((1,), (1,)), ((), ()))
    scale = HEAD_DIM ** -0.5 * LOG2_E

    def sub_tile(r0):
        sub = slice(r0, r0 + SUB_TILE)
        x = x_ref[sub, :]
        xn = _rms_norm(x, ng_ref[...]).astype(bf16)

        def proj(off, width):
            return jnp.dot(xn, win_ref[:, off:off + width], preferred_element_type=f32)

        def attention_side():
            cos_t, sin_t = cos_ref[sub, :], sin_ref[sub, :]

            def rope(t):
                return t * cos_t + pltpu.roll(t, HEAD_DIM, 1) * sin_t

            q = proj(OFF_Q, ATTN_WIDTH)
            for c in range(ATTN_WIDTH // LANES):
                qc = rope(q[:, c * LANES:(c + 1) * LANES]) * scale
                q_buf[sub, c * LANES:(c + 1) * LANES] = qc.astype(bf16)
                q_buf[sub, ATTN_WIDTH + c * LANES:ATTN_WIDTH + (c + 1) * LANES] = (
                    pltpu.roll(qc, LANES - HEAD_DIM // 2, 1).astype(bf16))
            yield

            kv = proj(OFF_KV, 2 * KV_WIDTH)
            new_rows = slice(BLOCK + r0, BLOCK + r0 + SUB_TILE)
            kr = rope(kv[:, 0:KV_WIDTH])
            kr_sw = pltpu.roll(kr, LANES - HEAD_DIM // 2, 1)
            kz_buf[0, new_rows, :] = jnp.where(first_head, kr, 0.0).astype(bf16)
            kz_buf[1, new_rows, :] = jnp.where(first_head, kr_sw, 0.0).astype(bf16)
            vc = kv[:, KV_WIDTH:2 * KV_WIDTH]
            vc_sw = pltpu.roll(vc, HEAD_DIM, 1)
            vx_buf[0, new_rows, 0:LANES] = jnp.where(low_half, vc, 1.0).astype(bf16)
            vx_buf[0, new_rows, LANES:2 * LANES] = jnp.where(low_half, 1.0, vc_sw).astype(bf16)
            vx_buf[1, new_rows, 0:LANES] = jnp.where(low_half, vc_sw, 1.0).astype(bf16)
            vx_buf[1, new_rows, LANES:2 * LANES] = jnp.where(low_half, 1.0, vc).astype(bf16)
            yield

            g_attn = _silu(proj(OFF_GA, ATTN_WIDTH))
            yield

            for n in range(SUB_TILE // BLOCK):
                rows = slice(r0 + n * BLOCK, r0 + (n + 1) * BLOCK)
                keys = slice(r0 + n * BLOCK, r0 + (n + 2) * BLOCK)
                loc = slice(n * BLOCK, (n + 1) * BLOCK)
                first = r0 == 0 and n == 0
                for g in range(N_KV_HEADS):
                    base = g * GROUP * HEAD_DIM
                    q4 = jnp.concatenate(
                        [q_buf[rows, off:off + LANES]
                         for off in (base, base + LANES, ATTN_WIDTH + base, ATTN_WIDTH + base + LANES)], axis=0)
                    vx = vx_buf[g, keys, :]
                    vx = jnp.concatenate(
                        [jnp.where(sink_value_mask, 0.0, vx[0:BF16_ROWS]).astype(bf16), vx[BF16_ROWS:]], axis=0)
                    kz = kz_buf[g, keys, :]
                    kz = jnp.concatenate(
                        [jnp.where(sink_key_mask, 0.0, kz[0:BF16_ROWS]).astype(bf16), kz[BF16_ROWS:]], axis=0)
                    s = lax.dot_general(q4, kz, nt_dims, preferred_element_type=f32)
                    p = []
                    for slab, head in enumerate((0, 2, 1, 3)):
                        sc = s[slab * BLOCK:(slab + 1) * BLOCK]
                        s_prev = sc[:, 0:BLOCK] + prev_bias(g * GROUP + head, first)
                        s_cur = sc[:, BLOCK:2 * BLOCK] + bias_cur
                        m = jnp.max(jnp.maximum(s_prev, s_cur), axis=-1, keepdims=True)
                        p.append(jnp.concatenate(
                            [jnp.exp2(s_prev - m), jnp.exp2(s_cur - m)], axis=1).astype(bf16))
                    pv = jnp.dot(jnp.concatenate(p, axis=0), vx, preferred_element_type=f32)
                    for c in range(2):
                        lo = pv[c * BLOCK:(c + 1) * BLOCK]
                        hi = pv[(2 + c) * BLOCK:(3 + c) * BLOCK]
                        num = jnp.where(low_half_blk, lo[:, 0:LANES], hi[:, LANES:2 * LANES])
                        den = jnp.where(low_half_blk, lo[:, LANES:2 * LANES], hi[:, 0:LANES])
                        cols = slice(base + c * LANES, base + (c + 1) * LANES)
                        mix_buf[rows, cols] = (num / den * g_attn[loc, cols]).astype(bf16)
                    yield

        def conv_side():
            u = proj(OFF_C, CONV_WIDTH) * proj(OFF_H, CONV_WIDTH)
            u0 = CONV_PAD + r0
            u_buf[u0:u0 + SUB_TILE, :] = u
            conv = (convw_ref[0:1, :] * u_buf[u0 - 2:u0 - 2 + SUB_TILE, :]
                    + convw_ref[1:2, :] * u_buf[u0 - 1:u0 - 1 + SUB_TILE, :]
                    + convw_ref[2:3, :] * u)
            yield
            y_conv = proj(OFF_B, CONV_WIDTH) * conv
            yield
            y_conv = y_conv * _silu(proj(OFF_GC, CONV_WIDTH))
            mix_buf[sub, ATTN_WIDTH:ATTN_WIDTH + CONV_WIDTH] = y_conv.astype(bf16)
            yield

        def output():
            h = x + jnp.dot(mix_buf[sub, :], wout_ref[...], preferred_element_type=f32)
            out_ref[sub, :] = _rms_norm(h, fg_ref[...])
            yield

        return {"q": attention_side(), "c": conv_side(), "o": output()}

    assert tile == 2 * SUB_TILE and SUB_TILE == 4 * BLOCK
    streams = {"A": sub_tile(0), "B": sub_tile(SUB_TILE)}
    for token in TRACE_ORDER.split():
        next(streams[token[0]][token[1]])

    u_buf[0:CONV_PAD, :] = u_buf[tile:tile + CONV_PAD, :]
    kz_buf[:, 0:BLOCK, :] = kz_buf[:, tile:tile + BLOCK, :]
    vx_buf[:, 0:BLOCK, :] = vx_buf[:, tile:tile + BLOCK, :]


@jax.jit
def kernel(x, norm_g, w_in, sinks, conv_w, w_out, final_g):
    batch, seq, d_model = x.shape
    tile = SEQ_TILE
    assert d_model == D_MODEL and seq % tile == 0 and tile % SUB_TILE == 0 and SUB_TILE % BLOCK == 0
    in_width = w_in.shape[1]
    cos_t, sin_t = _rope_tables(seq)

    grid_spec = pltpu.PrefetchScalarGridSpec(
        num_scalar_prefetch=1,
        grid=(batch, seq // tile),
        in_specs=[
            pl.BlockSpec((None, tile, D_MODEL), lambda b, j, s: (b, j, 0)),
            pl.BlockSpec((1, D_MODEL), lambda b, j, s: (0, 0)),
            pl.BlockSpec((1, D_MODEL), lambda b, j, s: (0, 0)),
            pl.BlockSpec((D_MODEL, in_width), lambda b, j, s: (0, 0)),
            pl.BlockSpec((D_MODEL, D_MODEL), lambda b, j, s: (0, 0)),
            pl.BlockSpec((CONV_K, CONV_WIDTH), lambda b, j, s: (0, 0)),
            pl.BlockSpec((tile, LANES), lambda b, j, s: (j, 0)),
            pl.BlockSpec((tile, LANES), lambda b, j, s: (j, 0)),
        ],
        out_specs=pl.BlockSpec((None, tile, D_MODEL), lambda b, j, s: (b, j, 0)),
        scratch_shapes=[
            pltpu.VMEM((tile, 2 * ATTN_WIDTH), jnp.bfloat16),
            pltpu.VMEM((N_KV_HEADS, BLOCK + tile, LANES), jnp.bfloat16),
            pltpu.VMEM((N_KV_HEADS, BLOCK + tile, 2 * LANES), jnp.bfloat16),
            pltpu.VMEM((CONV_PAD + tile, CONV_WIDTH), jnp.float32),
            pltpu.VMEM((tile, D_MODEL), jnp.bfloat16),
        ],
    )
    return pl.pallas_call(
        _layer_kernel,
        grid_spec=grid_spec,
        out_shape=jax.ShapeDtypeStruct(x.shape, x.dtype),
        compiler_params=pltpu.CompilerParams(
            dimension_semantics=("arbitrary", "arbitrary"),
            vmem_limit_bytes=VMEM_LIMIT_BYTES),
        name="hybrid_layer",
    )(sinks.astype(jnp.float32), x, norm_g.reshape(1, D_MODEL), final_g.reshape(1, D_MODEL),
      _permute_qk_columns(w_in).astype(jnp.bfloat16), w_out.astype(jnp.bfloat16), conv_w,
      cos_t, sin_t)
```

```python
import functools

import jax
import jax.numpy as jnp
import numpy as np
from jax import lax
from jax.experimental import pallas as pl
from jax.experimental.pallas import tpu as pltpu

D_MODEL = 1024
HEAD_DIM = 64
N_Q_HEADS = 8
N_KV_HEADS = 2
GROUP = N_Q_HEADS // N_KV_HEADS
ATTN_WIDTH = N_Q_HEADS * HEAD_DIM
KV_WIDTH = N_KV_HEADS * HEAD_DIM
BLOCK = 128
ROT_DIM = HEAD_DIM // 4
ROPE_THETA = 500000.0
CONV_WIDTH = D_MODEL - ATTN_WIDTH
CONV_K = 3
EPS = 1e-5
LOG2_E = 1.4426950408889634

LANES = 128
BF16_ROWS = 16
CONV_PAD = 8
SMALL_ROWS = 8
SEQ_TILE = 1024
SUB_TILE = 512
VMEM_LIMIT_BYTES = 56 * 1024 * 1024
TRACE_ORDER = ("Aq Aq Aq Ac Ac Ac "
               "Aq Bq Aq Bq Aq Bq Aq Bc Aq Bc Aq Bc Aq Aq "
               "Bq Bq Ao Bq Bq Bq Bq Bq Bq Bo")

OFF_Q = 0
OFF_KV = OFF_Q + ATTN_WIDTH
OFF_GA = OFF_KV + 2 * KV_WIDTH
OFF_B = OFF_GA + ATTN_WIDTH
OFF_C = OFF_B + CONV_WIDTH
OFF_H = OFF_C + CONV_WIDTH
OFF_GC = OFF_H + CONV_WIDTH


@functools.lru_cache(maxsize=None)
def _rope_tables(seq):
    half = ROT_DIM // 2
    inv_freq = ROPE_THETA ** (-np.arange(0, ROT_DIM, 2, dtype=np.float64) / ROT_DIM)
    ang = np.arange(seq, dtype=np.float64)[:, None] * inv_freq[None, :]
    cos, sin = np.cos(ang), np.sin(ang)
    ones = np.ones((seq, HEAD_DIM - ROT_DIM))
    zeros_h = np.zeros((seq, half))
    zeros_t = np.zeros((seq, HEAD_DIM - ROT_DIM))
    cos_t = np.concatenate([cos, cos, ones], axis=1)
    sin_lo = np.concatenate([-sin, zeros_h, zeros_t], axis=1)
    sin_hi = np.concatenate([zeros_h, sin, zeros_t], axis=1)
    rep = LANES // HEAD_DIM
    return tuple(np.tile(t, (1, rep)).astype(np.float32) for t in (cos_t, sin_lo, sin_hi))


def _rms_norm(x, g):
    ms = jnp.mean(x * x, axis=-1, keepdims=True)
    return x * lax.rsqrt(ms + EPS) * g


def _silu(x):
    h = 0.5 * x
    return h + h * jnp.tanh(h)


def _layer_kernel(sinks_ref, x_ref, par_ref, win_ref, wout_ref,
                  cos_ref, sinlo_ref, sinhi_ref, out_ref,
                  q_buf, kz_buf, vx_buf, u_buf, mix_buf):
    tile = x_ref.shape[0]
    j = pl.program_id(1)
    norm_gain, final_gain = par_ref[0:1, :], par_ref[1:2, :]
    taps = [par_ref[2 + k:3 + k, 0:CONV_WIDTH] for k in range(CONV_K)]
    f32, bf16 = jnp.float32, jnp.bfloat16

    @pl.when(j == 0)
    def _():
        kz_buf[:, 0:BLOCK, :] = jnp.zeros((N_KV_HEADS, BLOCK, LANES), bf16)
        vx_buf[:, 0:BLOCK, :] = jnp.ones((N_KV_HEADS, BLOCK, 2 * LANES), bf16)
        u_buf[0:CONV_PAD, :] = jnp.zeros((CONV_PAD, CONV_WIDTH), f32)

    low_half = lax.broadcasted_iota(jnp.int32, (SUB_TILE, LANES), 1) < HEAD_DIM
    qi = lax.broadcasted_iota(jnp.int32, (BLOCK, BLOCK), 0)
    kj = lax.broadcasted_iota(jnp.int32, (BLOCK, BLOCK), 1)
    neg_inf = jnp.float32(-jnp.inf)
    bias_cur = jnp.where(kj <= qi, 0.0, neg_inf)
    bias_prev_mid = jnp.where(kj > qi, 0.0, neg_inf)
    bias_prev_first = jnp.where((kj > qi) & (j > 0), 0.0, neg_inf)
    sink_slot = kj == 0
    bias_prev_sink = {}

    def prev_bias(head, first):
        if (head, first) not in bias_prev_sink:
            bias_prev_sink[head, first] = jnp.where(
                sink_slot, sinks_ref[head] * LOG2_E, bias_prev_first if first else bias_prev_mid)
        return bias_prev_sink[head, first]

    low_half_blk = kj < HEAD_DIM
    top_r = lax.broadcasted_iota(jnp.int32, (BF16_ROWS, 2 * LANES), 0)
    top_c = lax.broadcasted_iota(jnp.int32, (BF16_ROWS, 2 * LANES), 1)
    sink_value_mask = (top_r == 0) & ((top_c < HEAD_DIM) | (top_c >= 2 * LANES - HEAD_DIM))
    sink_key_mask = lax.broadcasted_iota(jnp.int32, (BF16_ROWS, LANES), 0) == 0
    nt_dims = (((1,), (1,)), ((), ()))
    scale = HEAD_DIM ** -0.5 * LOG2_E

    def sub_tile(r0):
        sub = slice(r0, r0 + SUB_TILE)
        x = x_ref[sub, :]
        xn = _rms_norm(x, norm_gain).astype(bf16)

        def proj(off, width):
            return jnp.dot(xn, win_ref[:, off:off + width], preferred_element_type=f32)

        def attention_side():
            cos_t, sin_lo, sin_hi = cos_ref[sub, :], sinlo_ref[sub, :], sinhi_ref[sub, :]

            def rope(t):
                return (t * cos_t + pltpu.roll(t, LANES - ROT_DIM // 2, 1) * sin_lo
                        + pltpu.roll(t, ROT_DIM // 2, 1) * sin_hi)

            q = proj(OFF_Q, ATTN_WIDTH)
            for c in range(ATTN_WIDTH // LANES):
                qc = rope(q[:, c * LANES:(c + 1) * LANES]) * scale
                q_buf[sub, c * LANES:(c + 1) * LANES] = qc.astype(bf16)
                q_buf[sub, ATTN_WIDTH + c * LANES:ATTN_WIDTH + (c + 1) * LANES] = (
                    pltpu.roll(qc, HEAD_DIM, 1).astype(bf16))
            yield

            kv = proj(OFF_KV, 2 * KV_WIDTH)
            new_rows = slice(BLOCK + r0, BLOCK + r0 + SUB_TILE)
            kr = rope(kv[:, 0:KV_WIDTH])
            kr_sw = pltpu.roll(kr, HEAD_DIM, 1)
            kz_buf[0, new_rows, :] = jnp.where(low_half, kr, 0.0).astype(bf16)
            kz_buf[1, new_rows, :] = jnp.where(low_half, kr_sw, 0.0).astype(bf16)
            vc = kv[:, KV_WIDTH:2 * KV_WIDTH]
            vc_sw = pltpu.roll(vc, HEAD_DIM, 1)
            vx_buf[0, new_rows, 0:LANES] = jnp.where(low_half, vc, 1.0).astype(bf16)
            vx_buf[0, new_rows, LANES:2 * LANES] = jnp.where(low_half, 1.0, vc_sw).astype(bf16)
            vx_buf[1, new_rows, 0:LANES] = jnp.where(low_half, vc_sw, 1.0).astype(bf16)
            vx_buf[1, new_rows, LANES:2 * LANES] = jnp.where(low_half, 1.0, vc).astype(bf16)
            yield

            g_attn = _silu(proj(OFF_GA, ATTN_WIDTH))
            yield

            for n in range(SUB_TILE // BLOCK):
                rows = slice(r0 + n * BLOCK, r0 + (n + 1) * BLOCK)
                keys = slice(r0 + n * BLOCK, r0 + (n + 2) * BLOCK)
                loc = slice(n * BLOCK, (n + 1) * BLOCK)
                first = r0 == 0 and n == 0
                for g in range(N_KV_HEADS):
                    base = g * GROUP * HEAD_DIM
                    q4 = jnp.concatenate(
                        [q_buf[rows, off:off + LANES]
                         for off in (base, base + LANES, ATTN_WIDTH + base, ATTN_WIDTH + base + LANES)], axis=0)
                    vx = vx_buf[g, keys, :]
                    vx = jnp.concatenate(
                        [jnp.where(sink_value_mask, 0.0, vx[0:BF16_ROWS]).astype(bf16), vx[BF16_ROWS:]], axis=0)
                    kz = kz_buf[g, keys, :]
                    kz = jnp.concatenate(
                        [jnp.where(sink_key_mask, 0.0, kz[0:BF16_ROWS]).astype(bf16), kz[BF16_ROWS:]], axis=0)
                    s = lax.dot_general(q4, kz, nt_dims, preferred_element_type=f32)
                    p = []
                    for slab, head in enumerate((0, 2, 1, 3)):
                        sc = s[slab * BLOCK:(slab + 1) * BLOCK]
                        s_prev = sc[:, 0:BLOCK] + prev_bias(g * GROUP + head, first)
                        s_cur = sc[:, BLOCK:2 * BLOCK] + bias_cur
                        m = jnp.max(jnp.maximum(s_prev, s_cur), axis=-1, keepdims=True)
                        p.append(jnp.concatenate(
                            [jnp.exp2(s_prev - m), jnp.exp2(s_cur - m)], axis=1).astype(bf16))
                    pv = jnp.dot(jnp.concatenate(p, axis=0), vx, preferred_element_type=f32)
                    for c in range(2):
                        lo = pv[c * BLOCK:(c + 1) * BLOCK]
                        hi = pv[(2 + c) * BLOCK:(3 + c) * BLOCK]
                        num = jnp.where(low_half_blk, lo[:, 0:LANES], hi[:, LANES:2 * LANES])
                        den = jnp.where(low_half_blk, lo[:, LANES:2 * LANES], hi[:, 0:LANES])
                        cols = slice(base + c * LANES, base + (c + 1) * LANES)
                        mix_buf[rows, cols] = (num / den * g_attn[loc, cols]).astype(bf16)
                    yield

        def conv_side():
            u = proj(OFF_C, CONV_WIDTH) * proj(OFF_H, CONV_WIDTH)
            u0 = CONV_PAD + r0
            u_buf[u0:u0 + SUB_TILE, :] = u
            conv = (taps[0] * u_buf[u0 - 2:u0 - 2 + SUB_TILE, :]
                    + taps[1] * u_buf[u0 - 1:u0 - 1 + SUB_TILE, :]
                    + taps[2] * u)
            yield
            y_conv = proj(OFF_B, CONV_WIDTH) * conv
            yield
            y_conv = y_conv * _silu(proj(OFF_GC, CONV_WIDTH))
            mix_buf[sub, ATTN_WIDTH:ATTN_WIDTH + CONV_WIDTH] = y_conv.astype(bf16)
            yield

        def output():
            h = x + jnp.dot(mix_buf[sub, :], wout_ref[...], preferred_element_type=f32)
            out_ref[sub, :] = _rms_norm(h, final_gain)
            yield

        return {"q": attention_side(), "c": conv_side(), "o": output()}

    assert tile == 2 * SUB_TILE and SUB_TILE == 4 * BLOCK
    streams = {"A": sub_tile(0), "B": sub_tile(SUB_TILE)}
    for token in TRACE_ORDER.split():
        next(streams[token[0]][token[1]])

    u_buf[0:CONV_PAD, :] = u_buf[tile:tile + CONV_PAD, :]
    kz_buf[:, 0:BLOCK, :] = kz_buf[:, tile:tile + BLOCK, :]
    vx_buf[:, 0:BLOCK, :] = vx_buf[:, tile:tile + BLOCK, :]


def _pack_small_params(norm_g, final_g, conv_w):
    taps = jnp.pad(conv_w.astype(jnp.float32), ((0, 0), (0, D_MODEL - CONV_WIDTH)))
    rows = [norm_g.reshape(1, D_MODEL).astype(jnp.float32), final_g.reshape(1, D_MODEL).astype(jnp.float32), taps]
    pad = jnp.zeros((SMALL_ROWS - 2 - CONV_K, D_MODEL), jnp.float32)
    return jnp.concatenate(rows + [pad], axis=0)


@jax.jit
def kernel(x, norm_g, w_in, sinks, conv_w, w_out, final_g):
    batch, seq, d_model = x.shape
    tile = SEQ_TILE
    assert d_model == D_MODEL and seq % tile == 0 and tile % SUB_TILE == 0 and SUB_TILE % BLOCK == 0
    in_width = w_in.shape[1]
    cos_t, sin_lo, sin_hi = _rope_tables(seq)

    grid_spec = pltpu.PrefetchScalarGridSpec(
        num_scalar_prefetch=1,
        grid=(batch, seq // tile),
        in_specs=[
            pl.BlockSpec((None, tile, D_MODEL), lambda b, j, s: (b, j, 0)),
            pl.BlockSpec((SMALL_ROWS, D_MODEL), lambda b, j, s: (0, 0)),
            pl.BlockSpec((D_MODEL, in_width), lambda b, j, s: (0, 0)),
            pl.BlockSpec((D_MODEL, D_MODEL), lambda b, j, s: (0, 0)),
            pl.BlockSpec((tile, LANES), lambda b, j, s: (j, 0)),
            pl.BlockSpec((tile, LANES), lambda b, j, s: (j, 0)),
            pl.BlockSpec((tile, LANES), lambda b, j, s: (j, 0)),
        ],
        out_specs=pl.BlockSpec((None, tile, D_MODEL), lambda b, j, s: (b, j, 0)),
        scratch_shapes=[
            pltpu.VMEM((tile, 2 * ATTN_WIDTH), jnp.bfloat16),
            pltpu.VMEM((N_KV_HEADS, BLOCK + tile, LANES), jnp.bfloat16),
            pltpu.VMEM((N_KV_HEADS, BLOCK + tile, 2 * LANES), jnp.bfloat16),
            pltpu.VMEM((CONV_PAD + tile, CONV_WIDTH), jnp.float32),
            pltpu.VMEM((tile, D_MODEL), jnp.bfloat16),
        ],
    )
    return pl.pallas_call(
        _layer_kernel,
        grid_spec=grid_spec,
        out_shape=jax.ShapeDtypeStruct(x.shape, x.dtype),
        compiler_params=pltpu.CompilerParams(
            dimension_semantics=("arbitrary", "arbitrary"),
            vmem_limit_bytes=VMEM_LIMIT_BYTES),
        name="hybrid_layer",
    )(sinks.astype(jnp.float32), x, _pack_small_params(norm_g, final_g, conv_w),
      w_in.astype(jnp.bfloat16), w_out.astype(jnp.bfloat16),
      cos_t, sin_lo, sin_hi)
```

```python
import functools

import jax
import jax.numpy as jnp
import numpy as np
from jax import lax
from jax.experimental import pallas as pl
from jax.experimental.pallas import tpu as pltpu

D_MODEL = 1024
HEAD_DIM = 64
N_Q_HEADS = 8
N_KV_HEADS = 2
GROUP = N_Q_HEADS // N_KV_HEADS
ATTN_WIDTH = N_Q_HEADS * HEAD_DIM
KV_WIDTH = N_KV_HEADS * HEAD_DIM
BLOCK = 128
ROT_DIM = HEAD_DIM // 4
ROPE_THETA = 500000.0
CONV_WIDTH = D_MODEL - ATTN_WIDTH
CONV_K = 3
EPS = 1e-5
LOG2_E = 1.4426950408889634

LANES = 128
BF16_ROWS = 16
CONV_PAD = 8
SMALL_ROWS = 8
SEQ_TILE = 1024
SUB_TILE = 512
VMEM_LIMIT_BYTES = 56 * 1024 * 1024
TRACE_ORDER = ("Aq Aq Aq Ac Ac Ac "
               "Aq Bq Aq Aq Bq Aq Aq Bq Aq Aq Aq "
               "Bq Bc Ao Bq Bq Bc Bq Bq Bc Bq Bq Bq Bo")

OFF_Q = 0
OFF_KV = OFF_Q + ATTN_WIDTH
OFF_GA = OFF_KV + 2 * KV_WIDTH
OFF_B = OFF_GA + ATTN_WIDTH
OFF_C = OFF_B + CONV_WIDTH
OFF_H = OFF_C + CONV_WIDTH
OFF_GC = OFF_H + CONV_WIDTH


@functools.lru_cache(maxsize=None)
def _rope_tables(seq):
    half = ROT_DIM // 2
    inv_freq = ROPE_THETA ** (-np.arange(0, ROT_DIM, 2, dtype=np.float64) / ROT_DIM)
    ang = np.arange(seq, dtype=np.float64)[:, None] * inv_freq[None, :]
    cos, sin = np.cos(ang), np.sin(ang)
    ones = np.ones((seq, HEAD_DIM - ROT_DIM))
    zeros_h = np.zeros((seq, half))
    zeros_t = np.zeros((seq, HEAD_DIM - ROT_DIM))
    cos_t = np.concatenate([cos, cos, ones], axis=1)
    sin_lo = np.concatenate([-sin, zeros_h, zeros_t], axis=1)
    sin_hi = np.concatenate([zeros_h, sin, zeros_t], axis=1)
    rep = LANES // HEAD_DIM
    return tuple(np.tile(t, (1, rep)).astype(np.float32) for t in (cos_t, sin_lo, sin_hi))


def _rms_norm(x, g):
    ms = jnp.mean(x * x, axis=-1, keepdims=True)
    return x * lax.rsqrt(ms + EPS) * g


def _silu(x):
    h = 0.5 * x
    return h + h * jnp.tanh(h)


def _layer_kernel(sinks_ref, x_ref, par_ref, win_ref, wout_ref,
                  cos_ref, sinlo_ref, sinhi_ref, out_ref,
                  q_buf, kz_buf, vx_buf, u_buf, mix_buf):
    tile = x_ref.shape[0]
    j = pl.program_id(1)
    norm_gain, final_gain = par_ref[0:1, :], par_ref[1:2, :]
    taps = [par_ref[2 + k:3 + k, 0:CONV_WIDTH] for k in range(CONV_K)]
    f32, bf16 = jnp.float32, jnp.bfloat16

    @pl.when(j == 0)
    def _():
        kz_buf[:, 0:BLOCK, :] = jnp.zeros((N_KV_HEADS, BLOCK, LANES), bf16)
        vx_buf[:, 0:BLOCK, :] = jnp.ones((N_KV_HEADS, BLOCK, 2 * LANES), bf16)
        u_buf[0:CONV_PAD, :] = jnp.zeros((CONV_PAD, CONV_WIDTH), f32)

    low_half = lax.broadcasted_iota(jnp.int32, (SUB_TILE, LANES), 1) < HEAD_DIM
    qi = lax.broadcasted_iota(jnp.int32, (BLOCK, BLOCK), 0)
    kj = lax.broadcasted_iota(jnp.int32, (BLOCK, BLOCK), 1)
    neg_inf = jnp.float32(-jnp.inf)
    bias_cur = jnp.where(kj <= qi, 0.0, neg_inf)
    bias_prev_mid = jnp.where(kj > qi, 0.0, neg_inf)
    bias_prev_first = jnp.where((kj > qi) & (j > 0), 0.0, neg_inf)
    sink_slot = kj == 0
    bias_prev_sink = {}

    def prev_bias(head, first):
        if (head, first) not in bias_prev_sink:
            bias_prev_sink[head, first] = jnp.where(
                sink_slot, sinks_ref[head] * LOG2_E, bias_prev_first if first else bias_prev_mid)
        return bias_prev_sink[head, first]

    low_half_blk = kj < HEAD_DIM
    top_r = lax.broadcasted_iota(jnp.int32, (BF16_ROWS, 2 * LANES), 0)
    top_c = lax.broadcasted_iota(jnp.int32, (BF16_ROWS, 2 * LANES), 1)
    sink_value_mask = (top_r == 0) & ((top_c < HEAD_DIM) | (top_c >= 2 * LANES - HEAD_DIM))
    sink_key_mask = lax.broadcasted_iota(jnp.int32, (BF16_ROWS, LANES), 0) == 0
    nt_dims = (((1,), (1,)), ((), ()))
    scale = HEAD_DIM ** -0.5 * LOG2_E

    def sub_tile(r0):
        sub = slice(r0, r0 + SUB_TILE)
        x = x_ref[sub, :]
        xn = _rms_norm(x, norm_gain).astype(bf16)

        def proj(off, width):
            return jnp.dot(xn, win_ref[:, off:off + width], preferred_element_type=f32)

        def attention_side():
            cos_t, sin_lo, sin_hi = cos_ref[sub, :], sinlo_ref[sub, :], sinhi_ref[sub, :]

            def rope(t):
                return (t * cos_t + pltpu.roll(t, LANES - ROT_DIM // 2, 1) * sin_lo
                        + pltpu.roll(t, ROT_DIM // 2, 1) * sin_hi)

            q = proj(OFF_Q, ATTN_WIDTH)
            for c in range(ATTN_WIDTH // LANES):
                qc = rope(q[:, c * LANES:(c + 1) * LANES]) * scale
                q_buf[sub, c * LANES:(c + 1) * LANES] = qc.astype(bf16)
                q_buf[sub, ATTN_WIDTH + c * LANES:ATTN_WIDTH + (c + 1) * LANES] = (
                    pltpu.roll(qc, HEAD_DIM, 1).astype(bf16))
            yield

            kv = proj(OFF_KV, 2 * KV_WIDTH)
            new_rows = slice(BLOCK + r0, BLOCK + r0 + SUB_TILE)
            kr = rope(kv[:, 0:KV_WIDTH])
            kr_sw = pltpu.roll(kr, HEAD_DIM, 1)
            kz_buf[0, new_rows, :] = jnp.where(low_half, kr, 0.0).astype(bf16)
            kz_buf[1, new_rows, :] = jnp.where(low_half, kr_sw, 0.0).astype(bf16)
            vc = kv[:, KV_WIDTH:2 * KV_WIDTH]
            vc_sw = pltpu.roll(vc, HEAD_DIM, 1)
            vx_buf[0, new_rows, 0:LANES] = jnp.where(low_half, vc, 1.0).astype(bf16)
            vx_buf[0, new_rows, LANES:2 * LANES] = jnp.where(low_half, 1.0, vc_sw).astype(bf16)
            vx_buf[1, new_rows, 0:LANES] = jnp.where(low_half, vc_sw, 1.0).astype(bf16)
            vx_buf[1, new_rows, LANES:2 * LANES] = jnp.where(low_half, 1.0, vc).astype(bf16)
            yield

            g_attn = _silu(proj(OFF_GA, ATTN_WIDTH))
            yield

            for n in range(SUB_TILE // BLOCK):
                rows = slice(r0 + n * BLOCK, r0 + (n + 1) * BLOCK)
                keys = slice(r0 + n * BLOCK, r0 + (n + 2) * BLOCK)
                loc = slice(n * BLOCK, (n + 1) * BLOCK)
                first = r0 == 0 and n == 0
                for g in range(N_KV_HEADS):
                    base = g * GROUP * HEAD_DIM
                    q4 = jnp.concatenate(
                        [q_buf[rows, off:off + LANES]
                         for off in (base, base + LANES, ATTN_WIDTH + base, ATTN_WIDTH + base + LANES)], axis=0)
                    vx = vx_buf[g, keys, :]
                    vx = jnp.concatenate(
                        [jnp.where(sink_value_mask, 0.0, vx[0:BF16_ROWS]).astype(bf16), vx[BF16_ROWS:]], axis=0)
                    kz = kz_buf[g, keys, :]
                    kz = jnp.concatenate(
                        [jnp.where(sink_key_mask, 0.0, kz[0:BF16_ROWS]).astype(bf16), kz[BF16_ROWS:]], axis=0)
                    s = lax.dot_general(q4, kz, nt_dims, preferred_element_type=f32)
                    p = []
                    for slab, head in enumerate((0, 2, 1, 3)):
                        sc = s[slab * BLOCK:(slab + 1) * BLOCK]
                        s_prev = sc[:, 0:BLOCK] + prev_bias(g * GROUP + head, first)
                        s_cur = sc[:, BLOCK:2 * BLOCK] + bias_cur
                        m = jnp.max(jnp.maximum(s_prev, s_cur), axis=-1, keepdims=True)
                        p.append(jnp.concatenate(
                            [jnp.exp2(s_prev - m), jnp.exp2(s_cur - m)], axis=1).astype(bf16))
                    pv = jnp.dot(jnp.concatenate(p, axis=0), vx, preferred_element_type=f32)
                    for c in range(2):
                        lo = pv[c * BLOCK:(c + 1) * BLOCK]
                        hi = pv[(2 + c) * BLOCK:(3 + c) * BLOCK]
                        num = jnp.where(low_half_blk, lo[:, 0:LANES], hi[:, LANES:2 * LANES])
                        den = jnp.where(low_half_blk, lo[:, LANES:2 * LANES], hi[:, 0:LANES])
                        cols = slice(base + c * LANES, base + (c + 1) * LANES)
                        mix_buf[rows, cols] = (num / den * g_attn[loc, cols]).astype(bf16)
                    yield

        def conv_side():
            u = proj(OFF_C, CONV_WIDTH) * proj(OFF_H, CONV_WIDTH)
            u0 = CONV_PAD + r0
            u_buf[u0:u0 + SUB_TILE, :] = u
            conv = (taps[0] * u_buf[u0 - 2:u0 - 2 + SUB_TILE, :]
                    + taps[1] * u_buf[u0 - 1:u0 - 1 + SUB_TILE, :]
                    + taps[2] * u)
            yield
            y_conv = proj(OFF_B, CONV_WIDTH) * conv
            yield
            y_conv = y_conv * _silu(proj(OFF_GC, CONV_WIDTH))
            mix_buf[sub, ATTN_WIDTH:ATTN_WIDTH + CONV_WIDTH] = y_conv.astype(bf16)
            yield

        def output():
            h = x + jnp.dot(mix_buf[sub, :], wout_ref[...], preferred_element_type=f32)
            out_ref[sub, :] = _rms_norm(h, final_gain)
            yield

        return {"q": attention_side(), "c": conv_side(), "o": output()}

    assert tile == 2 * SUB_TILE and SUB_TILE == 4 * BLOCK
    streams = {"A": sub_tile(0), "B": sub_tile(SUB_TILE)}
    for token in TRACE_ORDER.split():
        next(streams[token[0]][token[1]])

    u_buf[0:CONV_PAD, :] = u_buf[tile:tile + CONV_PAD, :]
    kz_buf[:, 0:BLOCK, :] = kz_buf[:, tile:tile + BLOCK, :]
    vx_buf[:, 0:BLOCK, :] = vx_buf[:, tile:tile + BLOCK, :]


def _pack_small_params(norm_g, final_g, conv_w):
    taps = jnp.pad(conv_w.astype(jnp.float32), ((0, 0), (0, D_MODEL - CONV_WIDTH)))
    rows = [norm_g.reshape(1, D_MODEL).astype(jnp.float32), final_g.reshape(1, D_MODEL).astype(jnp.float32), taps]
    pad = jnp.zeros((SMALL_ROWS - 2 - CONV_K, D_MODEL), jnp.float32)
    return jnp.concatenate(rows + [pad], axis=0)


@jax.jit
def kernel(x, norm_g, w_in, sinks, conv_w, w_out, final_g):
    batch, seq, d_model = x.shape
    tile = SEQ_TILE
    assert d_model == D_MODEL and seq % tile == 0 and tile % SUB_TILE == 0 and SUB_TILE % BLOCK == 0
    in_width = w_in.shape[1]
    cos_t, sin_lo, sin_hi = _rope_tables(seq)

    grid_spec = pltpu.PrefetchScalarGridSpec(
        num_scalar_prefetch=1,
        grid=(batch, seq // tile),
        in_specs=[
            pl.BlockSpec((None, tile, D_MODEL), lambda b, j, s: (b, j, 0)),
            pl.BlockSpec((SMALL_ROWS, D_MODEL), lambda b, j, s: (0, 0)),
            pl.BlockSpec((D_MODEL, in_width), lambda b, j, s: (0, 0)),
            pl.BlockSpec((D_MODEL, D_MODEL), lambda b, j, s: (0, 0)),
            pl.BlockSpec((tile, LANES), lambda b, j, s: (j, 0)),
            pl.BlockSpec((tile, LANES), lambda b, j, s: (j, 0)),
            pl.BlockSpec((tile, LANES), lambda b, j, s: (j, 0)),
        ],
        out_specs=pl.BlockSpec((None, tile, D_MODEL), lambda b, j, s: (b, j, 0)),
        scratch_shapes=[
            pltpu.VMEM((tile, 2 * ATTN_WIDTH), jnp.bfloat16),
            pltpu.VMEM((N_KV_HEADS, BLOCK + tile, LANES), jnp.bfloat16),
            pltpu.VMEM((N_KV_HEADS, BLOCK + tile, 2 * LANES), jnp.bfloat16),
            pltpu.VMEM((CONV_PAD + tile, CONV_WIDTH), jnp.float32),
            pltpu.VMEM((tile, D_MODEL), jnp.bfloat16),
        ],
    )
    return pl.pallas_call(
        _layer_kernel,
        grid_spec=grid_spec,
        out_shape=jax.ShapeDtypeStruct(x.shape, x.dtype),
        compiler_params=pltpu.CompilerParams(
            dimension_semantics=("arbitrary", "arbitrary"),
            vmem_limit_bytes=VMEM_LIMIT_BYTES),
        name="hybrid_layer",
    )(sinks.astype(jnp.float32), x, _pack_small_params(norm_g, final_g, conv_w),
      w_in.astype(jnp.bfloat16), w_out.astype(jnp.bfloat16),
      cos_t, sin_lo, sin_hi)
```

```python
import functools

import jax
import jax.numpy as jnp
import numpy as np
from jax import lax
from jax.experimental import pallas as pl
from jax.experimental.pallas import tpu as pltpu

D_MODEL = 1024
HEAD_DIM = 64
N_Q_HEADS = 8
N_KV_HEADS = 2
GROUP = N_Q_HEADS // N_KV_HEADS
ATTN_WIDTH = N_Q_HEADS * HEAD_DIM
KV_WIDTH = N_KV_HEADS * HEAD_DIM
BLOCK = 128
ROT_DIM = HEAD_DIM // 4
ROPE_THETA = 500000.0
CONV_WIDTH = D_MODEL - ATTN_WIDTH
CONV_K = 3
EPS = 1e-5
LOG2_E = 1.4426950408889634

LANES = 128
BF16_ROWS = 16
CONV_PAD = 8
SMALL_ROWS = 8
W_CHUNK = 256
SEQ_TILE = 1024
SUB_TILE = 512
VMEM_LIMIT_BYTES = 56 * 1024 * 1024
TRACE_ORDER = ("Aq Aq Aq Ac Ac Ac "
               "Aq Bq Aq Bq Aq Bq Aq Bc Aq Bc Aq Bc Aq Aq "
               "Bq Bq Ao Bq Bq Bq Bq Bq Bq Bo")

OFF_Q = 0
OFF_KV = OFF_Q + ATTN_WIDTH
OFF_GA = OFF_KV + 2 * KV_WIDTH
OFF_B = OFF_GA + ATTN_WIDTH
OFF_C = OFF_B + CONV_WIDTH
OFF_H = OFF_C + CONV_WIDTH
OFF_GC = OFF_H + CONV_WIDTH


@functools.lru_cache(maxsize=None)
def _rope_tables(seq):
    half = ROT_DIM // 2
    inv_freq = ROPE_THETA ** (-np.arange(0, ROT_DIM, 2, dtype=np.float64) / ROT_DIM)
    ang = np.arange(seq, dtype=np.float64)[:, None] * inv_freq[None, :]
    cos, sin = np.cos(ang), np.sin(ang)
    ones = np.ones((seq, HEAD_DIM - ROT_DIM))
    zeros_h = np.zeros((seq, half))
    zeros_t = np.zeros((seq, HEAD_DIM - ROT_DIM))
    cos_t = np.concatenate([cos, cos, ones], axis=1)
    sin_lo = np.concatenate([-sin, zeros_h, zeros_t], axis=1)
    sin_hi = np.concatenate([zeros_h, sin, zeros_t], axis=1)
    rep = LANES // HEAD_DIM
    return tuple(np.tile(t, (1, rep)).astype(np.float32) for t in (cos_t, sin_lo, sin_hi))


def _rms_norm(x, g):
    ms = jnp.mean(x * x, axis=-1, keepdims=True)
    return x * lax.rsqrt(ms + EPS) * g


def _silu(x):
    h = 0.5 * x
    return h + h * jnp.tanh(h)


def _layer_kernel(sinks_ref, x_ref, par_ref, win_hbm, wout_hbm,
                  cos_ref, sinlo_ref, sinhi_ref, out_ref,
                  q_buf, kz_buf, vx_buf, u_buf, mix_buf, win_ref, wout_ref, w_stage, w_sem):
    tile = x_ref.shape[0]
    j = pl.program_id(1)

    @pl.when((pl.program_id(0) == 0) & (j == 0))
    def _():
        chunks = ([(win_hbm, win_ref, c) for c in range(win_ref.shape[1] // W_CHUNK)]
                  + [(wout_hbm, wout_ref, c) for c in range(wout_ref.shape[1] // W_CHUNK)])

        def chunk_copy(i):
            src, _, c = chunks[i]
            return pltpu.make_async_copy(src.at[:, pl.ds(c * W_CHUNK, W_CHUNK)], w_stage.at[i % 2], w_sem.at[i % 2])

        chunk_copy(0).start()
        for i, (_, dst, c) in enumerate(chunks):
            if i + 1 < len(chunks):
                chunk_copy(i + 1).start()
            chunk_copy(i).wait()
            dst[:, c * W_CHUNK:(c + 1) * W_CHUNK] = w_stage[i % 2].astype(jnp.bfloat16)

    norm_gain, final_gain = par_ref[0:1, :], par_ref[1:2, :]
    taps = [par_ref[2 + k:3 + k, 0:CONV_WIDTH] for k in range(CONV_K)]
    f32, bf16 = jnp.float32, jnp.bfloat16

    @pl.when(j == 0)
    def _():
        kz_buf[:, 0:BLOCK, :] = jnp.zeros((N_KV_HEADS, BLOCK, LANES), bf16)
        vx_buf[:, 0:BLOCK, :] = jnp.ones((N_KV_HEADS, BLOCK, 2 * LANES), bf16)
        u_buf[0:CONV_PAD, :] = jnp.zeros((CONV_PAD, CONV_WIDTH), f32)

    low_half = lax.broadcasted_iota(jnp.int32, (SUB_TILE, LANES), 1) < HEAD_DIM
    qi = lax.broadcasted_iota(jnp.int32, (BLOCK, BLOCK), 0)
    kj = lax.broadcasted_iota(jnp.int32, (BLOCK, BLOCK), 1)
    neg_inf = jnp.float32(-jnp.inf)
    bias_cur = jnp.where(kj <= qi, 0.0, neg_inf)
    bias_prev_mid = jnp.where(kj > qi, 0.0, neg_inf)
    bias_prev_first = jnp.where((kj > qi) & (j > 0), 0.0, neg_inf)
    sink_slot = kj == 0
    bias_prev_sink = {}

    def prev_bias(head, first):
        if (head, first) not in bias_prev_sink:
            bias_prev_sink[head, first] = jnp.where(
                sink_slot, sinks_ref[head] * LOG2_E, bias_prev_first if first else bias_prev_mid)
        return bias_prev_sink[head, first]

    low_half_blk = kj < HEAD_DIM
    top_r = lax.broadcasted_iota(jnp.int32, (BF16_ROWS, 2 * LANES), 0)
    top_c = lax.broadcasted_iota(jnp.int32, (BF16_ROWS, 2 * LANES), 1)
    sink_value_mask = (top_r == 0) & ((top_c < HEAD_DIM) | (top_c >= 2 * LANES - HEAD_DIM))
    sink_key_mask = lax.broadcasted_iota(jnp.int32, (BF16_ROWS, LANES), 0) == 0
    nt_dims = (((1,), (1,)), ((), ()))
    scale = HEAD_DIM ** -0.5 * LOG2_E

    def sub_tile(r0):
        sub = slice(r0, r0 + SUB_TILE)
        x = x_ref[sub, :]
        xn = _rms_norm(x, norm_gain).astype(bf16)

        def proj(off, width):
            return jnp.dot(xn, win_ref[:, off:off + width], preferred_element_type=f32)

        def attention_side():
            cos_t, sin_lo, sin_hi = cos_ref[sub, :], sinlo_ref[sub, :], sinhi_ref[sub, :]

            def rope(t):
                return (t * cos_t + pltpu.roll(t, LANES - ROT_DIM // 2, 1) * sin_lo
                        + pltpu.roll(t, ROT_DIM // 2, 1) * sin_hi)

            q = proj(OFF_Q, ATTN_WIDTH)
            for c in range(ATTN_WIDTH // LANES):
                qc = rope(q[:, c * LANES:(c + 1) * LANES]) * scale
                q_buf[sub, c * LANES:(c + 1) * LANES] = qc.astype(bf16)
                q_buf[sub, ATTN_WIDTH + c * LANES:ATTN_WIDTH + (c + 1) * LANES] = (
                    pltpu.roll(qc, HEAD_DIM, 1).astype(bf16))
            yield

            kv = proj(OFF_KV, 2 * KV_WIDTH)
            new_rows = slice(BLOCK + r0, BLOCK + r0 + SUB_TILE)
            kr = rope(kv[:, 0:KV_WIDTH])
            kr_sw = pltpu.roll(kr, HEAD_DIM, 1)
            kz_buf[0, new_rows, :] = jnp.where(low_half, kr, 0.0).astype(bf16)
            kz_buf[1, new_rows, :] = jnp.where(low_half, kr_sw, 0.0).astype(bf16)
            vc = kv[:, KV_WIDTH:2 * KV_WIDTH]
            vc_sw = pltpu.roll(vc, HEAD_DIM, 1)
            vx_buf[0, new_rows, 0:LANES] = jnp.where(low_half, vc, 1.0).astype(bf16)
            vx_buf[0, new_rows, LANES:2 * LANES] = jnp.where(low_half, 1.0, vc_sw).astype(bf16)
            vx_buf[1, new_rows, 0:LANES] = jnp.where(low_half, vc_sw, 1.0).astype(bf16)
            vx_buf[1, new_rows, LANES:2 * LANES] = jnp.where(low_half, 1.0, vc).astype(bf16)
            yield

            g_attn = _silu(proj(OFF_GA, ATTN_WIDTH))
            yield

            for n in range(SUB_TILE // BLOCK):
                rows = slice(r0 + n * BLOCK, r0 + (n + 1) * BLOCK)
                keys = slice(r0 + n * BLOCK, r0 + (n + 2) * BLOCK)
                loc = slice(n * BLOCK, (n + 1) * BLOCK)
                first = r0 == 0 and n == 0
                for g in range(N_KV_HEADS):
                    base = g * GROUP * HEAD_DIM
                    q4 = jnp.concatenate(
                        [q_buf[rows, off:off + LANES]
                         for off in (base, base + LANES, ATTN_WIDTH + base, ATTN_WIDTH + base + LANES)], axis=0)
                    vx = vx_buf[g, keys, :]
                    vx = jnp.concatenate(
                        [jnp.where(sink_value_mask, 0.0, vx[0:BF16_ROWS]).astype(bf16), vx[BF16_ROWS:]], axis=0)
                    kz = kz_buf[g, keys, :]
                    kz = jnp.concatenate(
                        [jnp.where(sink_key_mask, 0.0, kz[0:BF16_ROWS]).astype(bf16), kz[BF16_ROWS:]], axis=0)
                    s = lax.dot_general(q4, kz, nt_dims, preferred_element_type=f32)
                    p = []
                    for slab, head in enumerate((0, 2, 1, 3)):
                        sc = s[slab * BLOCK:(slab + 1) * BLOCK]
                        s_prev = sc[:, 0:BLOCK] + prev_bias(g * GROUP + head, first)
                        s_cur = sc[:, BLOCK:2 * BLOCK] + bias_cur
                        m = jnp.max(jnp.maximum(s_prev, s_cur), axis=-1, keepdims=True)
                        p.append(jnp.concatenate(
                            [jnp.exp2(s_prev - m), jnp.exp2(s_cur - m)], axis=1).astype(bf16))
                    pv = jnp.dot(jnp.concatenate(p, axis=0), vx, preferred_element_type=f32)
                    for c in range(2):
                        lo = pv[c * BLOCK:(c + 1) * BLOCK]
                        hi = pv[(2 + c) * BLOCK:(3 + c) * BLOCK]
                        num = jnp.where(low_half_blk, lo[:, 0:LANES], hi[:, LANES:2 * LANES])
                        den = jnp.where(low_half_blk, lo[:, LANES:2 * LANES], hi[:, 0:LANES])
                        cols = slice(base + c * LANES, base + (c + 1) * LANES)
                        mix_buf[rows, cols] = (num / den * g_attn[loc, cols]).astype(bf16)
                    yield

        def conv_side():
            u = proj(OFF_C, CONV_WIDTH) * proj(OFF_H, CONV_WIDTH)
            u0 = CONV_PAD + r0
            u_buf[u0:u0 + SUB_TILE, :] = u
            conv = (taps[0] * u_buf[u0 - 2:u0 - 2 + SUB_TILE, :]
                    + taps[1] * u_buf[u0 - 1:u0 - 1 + SUB_TILE, :]
                    + taps[2] * u)
            yield
            y_conv = proj(OFF_B, CONV_WIDTH) * conv
            yield
            y_conv = y_conv * _silu(proj(OFF_GC, CONV_WIDTH))
            mix_buf[sub, ATTN_WIDTH:ATTN_WIDTH + CONV_WIDTH] = y_conv.astype(bf16)
            yield

        def output():
            h = x + jnp.dot(mix_buf[sub, :], wout_ref[...], preferred_element_type=f32)
            out_ref[sub, :] = _rms_norm(h, final_gain)
            yield

        return {"q": attention_side(), "c": conv_side(), "o": output()}

    assert tile == 2 * SUB_TILE and SUB_TILE == 4 * BLOCK
    streams = {"A": sub_tile(0), "B": sub_tile(SUB_TILE)}
    for token in TRACE_ORDER.split():
        next(streams[token[0]][token[1]])

    u_buf[0:CONV_PAD, :] = u_buf[tile:tile + CONV_PAD, :]
    kz_buf[:, 0:BLOCK, :] = kz_buf[:, tile:tile + BLOCK, :]
    vx_buf[:, 0:BLOCK, :] = vx_buf[:, tile:tile + BLOCK, :]


def _pack_small_params(norm_g, final_g, conv_w):
    taps = jnp.pad(conv_w.astype(jnp.float32), ((0, 0), (0, D_MODEL - CONV_WIDTH)))
    rows = [norm_g.reshape(1, D_MODEL).astype(jnp.float32), final_g.reshape(1, D_MODEL).astype(jnp.float32), taps]
    pad = jnp.zeros((SMALL_ROWS - 2 - CONV_K, D_MODEL), jnp.float32)
    return jnp.concatenate(rows + [pad], axis=0)


@jax.jit
def kernel(x, norm_g, w_in, sinks, conv_w, w_out, final_g):
    batch, seq, d_model = x.shape
    tile = SEQ_TILE
    assert d_model == D_MODEL and seq % tile == 0 and tile % SUB_TILE == 0 and SUB_TILE % BLOCK == 0
    in_width = w_in.shape[1]
    cos_t, sin_lo, sin_hi = _rope_tables(seq)

    grid_spec = pltpu.PrefetchScalarGridSpec(
        num_scalar_prefetch=1,
        grid=(batch, seq // tile),
        in_specs=[
            pl.BlockSpec((None, tile, D_MODEL), lambda b, j, s: (b, j, 0)),
            pl.BlockSpec((SMALL_ROWS, D_MODEL), lambda b, j, s: (0, 0)),
            pl.BlockSpec(memory_space=pl.ANY),
            pl.BlockSpec(memory_space=pl.ANY),
            pl.BlockSpec((tile, LANES), lambda b, j, s: (j, 0)),
            pl.BlockSpec((tile, LANES), lambda b, j, s: (j, 0)),
            pl.BlockSpec((tile, LANES), lambda b, j, s: (j, 0)),
        ],
        out_specs=pl.BlockSpec((None, tile, D_MODEL), lambda b, j, s: (b, j, 0)),
        scratch_shapes=[
            pltpu.VMEM((tile, 2 * ATTN_WIDTH), jnp.bfloat16),
            pltpu.VMEM((N_KV_HEADS, BLOCK + tile, LANES), jnp.bfloat16),
            pltpu.VMEM((N_KV_HEADS, BLOCK + tile, 2 * LANES), jnp.bfloat16),
            pltpu.VMEM((CONV_PAD + tile, CONV_WIDTH), jnp.float32),
            pltpu.VMEM((tile, D_MODEL), jnp.bfloat16),
            pltpu.VMEM((D_MODEL, in_width), jnp.bfloat16),
            pltpu.VMEM((D_MODEL, D_MODEL), jnp.bfloat16),
            pltpu.VMEM((2, D_MODEL, W_CHUNK), jnp.float32),
            pltpu.SemaphoreType.DMA((2,)),
        ],
    )
    return pl.pallas_call(
        _layer_kernel,
        grid_spec=grid_spec,
        out_shape=jax.ShapeDtypeStruct(x.shape, x.dtype),
        compiler_params=pltpu.CompilerParams(
            dimension_semantics=("arbitrary", "arbitrary"),
            vmem_limit_bytes=VMEM_LIMIT_BYTES),
        name="hybrid_layer",
    )(sinks.astype(jnp.float32), x, _pack_small_params(norm_g, final_g, conv_w),
      w_in.astype(jnp.float32), w_out.astype(jnp.float32),
      cos_t, sin_lo, sin_hi)
```

```python
import functools

import jax
import jax.numpy as jnp
import numpy as np
from jax import lax
from jax.experimental import pallas as pl
from jax.experimental.pallas import tpu as pltpu

D_MODEL = 1024
HEAD_DIM = 64
N_Q_HEADS = 8
N_KV_HEADS = 2
GROUP = N_Q_HEADS // N_KV_HEADS
ATTN_WIDTH = N_Q_HEADS * HEAD_DIM
KV_WIDTH = N_KV_HEADS * HEAD_DIM
BLOCK = 128
ROT_DIM = HEAD_DIM // 4
ROPE_THETA = 500000.0
CONV_WIDTH = D_MODEL - ATTN_WIDTH
CONV_K = 3
EPS = 1e-5
LOG2_E = 1.4426950408889634

LANES = 128
BF16_ROWS = 16
CONV_PAD = 8
SMALL_ROWS = 8
SEQ_TILE = 1024
SUB_TILE = 256
VMEM_LIMIT_BYTES = 56 * 1024 * 1024
TRACE_ORDER = ("Aq Aq Aq Ac Ac Ac "
               "Aq Bq Aq Bq Aq Bq Aq Bc Aq Bc Aq Bc Aq Aq "
               "Bq Bq Ao Bq Bq Bq Bq Bq Bq Bo")

OFF_Q = 0
OFF_KV = OFF_Q + ATTN_WIDTH
OFF_GA = OFF_KV + 2 * KV_WIDTH
OFF_B = OFF_GA + ATTN_WIDTH
OFF_C = OFF_B + CONV_WIDTH
OFF_H = OFF_C + CONV_WIDTH
OFF_GC = OFF_H + CONV_WIDTH


@functools.lru_cache(maxsize=None)
def _rope_tables(seq):
    half = ROT_DIM // 2
    inv_freq = ROPE_THETA ** (-np.arange(0, ROT_DIM, 2, dtype=np.float64) / ROT_DIM)
    ang = np.arange(seq, dtype=np.float64)[:, None] * inv_freq[None, :]
    cos, sin = np.cos(ang), np.sin(ang)
    ones = np.ones((seq, HEAD_DIM - ROT_DIM))
    zeros_h = np.zeros((seq, half))
    zeros_t = np.zeros((seq, HEAD_DIM - ROT_DIM))
    cos_t = np.concatenate([cos, cos, ones], axis=1)
    sin_lo = np.concatenate([-sin, zeros_h, zeros_t], axis=1)
    sin_hi = np.concatenate([zeros_h, sin, zeros_t], axis=1)
    rep = LANES // HEAD_DIM
    return tuple(np.tile(t, (1, rep)).astype(np.float32) for t in (cos_t, sin_lo, sin_hi))


def _rms_norm(x, g):
    ms = jnp.mean(x * x, axis=-1, keepdims=True)
    return x * lax.rsqrt(ms + EPS) * g


def _silu(x):
    h = 0.5 * x
    return h + h * jnp.tanh(h)


def _layer_kernel(sinks_ref, x_ref, par_ref, win_ref, wout_ref,
                  cos_ref, sinlo_ref, sinhi_ref, out_ref,
                  q_buf, kz_buf, vx_buf, u_buf, mix_buf):
    tile = x_ref.shape[0]
    j = pl.program_id(1)
    norm_gain, final_gain = par_ref[0:1, :], par_ref[1:2, :]
    taps = [par_ref[2 + k:3 + k, 0:CONV_WIDTH] for k in range(CONV_K)]
    f32, bf16 = jnp.float32, jnp.bfloat16

    @pl.when(j == 0)
    def _():
        kz_buf[:, 0:BLOCK, :] = jnp.zeros((N_KV_HEADS, BLOCK, LANES), bf16)
        vx_buf[:, 0:BLOCK, :] = jnp.ones((N_KV_HEADS, BLOCK, 2 * LANES), bf16)
        u_buf[0:CONV_PAD, :] = jnp.zeros((CONV_PAD, CONV_WIDTH), f32)

    low_half = lax.broadcasted_iota(jnp.int32, (SUB_TILE, LANES), 1) < HEAD_DIM
    qi = lax.broadcasted_iota(jnp.int32, (BLOCK, BLOCK), 0)
    kj = lax.broadcasted_iota(jnp.int32, (BLOCK, BLOCK), 1)
    neg_inf = jnp.float32(-jnp.inf)
    bias_cur = jnp.where(kj <= qi, 0.0, neg_inf)
    bias_prev_mid = jnp.where(kj > qi, 0.0, neg_inf)
    bias_prev_first = jnp.where((kj > qi) & (j > 0), 0.0, neg_inf)
    sink_slot = kj == 0
    bias_prev_sink = {}

    def prev_bias(head, first):
        if (head, first) not in bias_prev_sink:
            bias_prev_sink[head, first] = jnp.where(
                sink_slot, sinks_ref[head] * LOG2_E, bias_prev_first if first else bias_prev_mid)
        return bias_prev_sink[head, first]

    low_half_blk = kj < HEAD_DIM
    top_r = lax.broadcasted_iota(jnp.int32, (BF16_ROWS, 2 * LANES), 0)
    top_c = lax.broadcasted_iota(jnp.int32, (BF16_ROWS, 2 * LANES), 1)
    sink_value_mask = (top_r == 0) & ((top_c < HEAD_DIM) | (top_c >= 2 * LANES - HEAD_DIM))
    sink_key_mask = lax.broadcasted_iota(jnp.int32, (BF16_ROWS, LANES), 0) == 0
    nt_dims = (((1,), (1,)), ((), ()))
    scale = HEAD_DIM ** -0.5 * LOG2_E

    def sub_tile(r0):
        sub = slice(r0, r0 + SUB_TILE)
        x = x_ref[sub, :]
        xn = _rms_norm(x, norm_gain).astype(bf16)

        def proj(off, width):
            return jnp.dot(xn, win_ref[:, off:off + width], preferred_element_type=f32)

        def attention_side():
            cos_t, sin_lo, sin_hi = cos_ref[sub, :], sinlo_ref[sub, :], sinhi_ref[sub, :]

            def rope(t):
                return (t * cos_t + pltpu.roll(t, LANES - ROT_DIM // 2, 1) * sin_lo
                        + pltpu.roll(t, ROT_DIM // 2, 1) * sin_hi)

            q = proj(OFF_Q, ATTN_WIDTH)
            for c in range(ATTN_WIDTH // LANES):
                qc = rope(q[:, c * LANES:(c + 1) * LANES]) * scale
                q_buf[sub, c * LANES:(c + 1) * LANES] = qc.astype(bf16)
                q_buf[sub, ATTN_WIDTH + c * LANES:ATTN_WIDTH + (c + 1) * LANES] = (
                    pltpu.roll(qc, HEAD_DIM, 1).astype(bf16))
            yield

            kv = proj(OFF_KV, 2 * KV_WIDTH)
            new_rows = slice(BLOCK + r0, BLOCK + r0 + SUB_TILE)
            kr = rope(kv[:, 0:KV_WIDTH])
            kr_sw = pltpu.roll(kr, HEAD_DIM, 1)
            kz_buf[0, new_rows, :] = jnp.where(low_half, kr, 0.0).astype(bf16)
            kz_buf[1, new_rows, :] = jnp.where(low_half, kr_sw, 0.0).astype(bf16)
            vc = kv[:, KV_WIDTH:2 * KV_WIDTH]
            vc_sw = pltpu.roll(vc, HEAD_DIM, 1)
            vx_buf[0, new_rows, 0:LANES] = jnp.where(low_half, vc, 1.0).astype(bf16)
            vx_buf[0, new_rows, LANES:2 * LANES] = jnp.where(low_half, 1.0, vc_sw).astype(bf16)
            vx_buf[1, new_rows, 0:LANES] = jnp.where(low_half, vc_sw, 1.0).astype(bf16)
            vx_buf[1, new_rows, LANES:2 * LANES] = jnp.where(low_half, 1.0, vc).astype(bf16)
            yield

            g_attn = _silu(proj(OFF_GA, ATTN_WIDTH))
            yield

            for n in range(SUB_TILE // BLOCK):
                rows = slice(r0 + n * BLOCK, r0 + (n + 1) * BLOCK)
                keys = slice(r0 + n * BLOCK, r0 + (n + 2) * BLOCK)
                loc = slice(n * BLOCK, (n + 1) * BLOCK)
                first = r0 == 0 and n == 0
                for g in range(N_KV_HEADS):
                    base = g * GROUP * HEAD_DIM
                    q4 = jnp.concatenate(
                        [q_buf[rows, off:off + LANES]
                         for off in (base, base + LANES, ATTN_WIDTH + base, ATTN_WIDTH + base + LANES)], axis=0)
                    vx = vx_buf[g, keys, :]
                    vx = jnp.concatenate(
                        [jnp.where(sink_value_mask, 0.0, vx[0:BF16_ROWS]).astype(bf16), vx[BF16_ROWS:]], axis=0)
                    kz = kz_buf[g, keys, :]
                    kz = jnp.concatenate(
                        [jnp.where(sink_key_mask, 0.0, kz[0:BF16_ROWS]).astype(bf16), kz[BF16_ROWS:]], axis=0)
                    s = lax.dot_general(q4, kz, nt_dims, preferred_element_type=f32)
                    p = []
                    for slab, head in enumerate((0, 2, 1, 3)):
                        sc = s[slab * BLOCK:(slab + 1) * BLOCK]
                        s_prev = sc[:, 0:BLOCK] + prev_bias(g * GROUP + head, first)
                        s_cur = sc[:, BLOCK:2 * BLOCK] + bias_cur
                        m = jnp.max(jnp.maximum(s_prev, s_cur), axis=-1, keepdims=True)
                        p.append(jnp.concatenate(
                            [jnp.exp2(s_prev - m), jnp.exp2(s_cur - m)], axis=1).astype(bf16))
                    pv = jnp.dot(jnp.concatenate(p, axis=0), vx, preferred_element_type=f32)
                    for c in range(2):
                        lo = pv[c * BLOCK:(c + 1) * BLOCK]
                        hi = pv[(2 + c) * BLOCK:(3 + c) * BLOCK]
                        num = jnp.where(low_half_blk, lo[:, 0:LANES], hi[:, LANES:2 * LANES])
                        den = jnp.where(low_half_blk, lo[:, LANES:2 * LANES], hi[:, 0:LANES])
                        cols = slice(base + c * LANES, base + (c + 1) * LANES)
                        mix_buf[rows, cols] = (num / den * g_attn[loc, cols]).astype(bf16)
                    yield

        def conv_side():
            u = proj(OFF_C, CONV_WIDTH) * proj(OFF_H, CONV_WIDTH)
            u0 = CONV_PAD + r0
            u_buf[u0:u0 + SUB_TILE, :] = u
            conv = (taps[0] * u_buf[u0 - 2:u0 - 2 + SUB_TILE, :]
                    + taps[1] * u_buf[u0 - 1:u0 - 1 + SUB_TILE, :]
                    + taps[2] * u)
            yield
            y_conv = proj(OFF_B, CONV_WIDTH) * conv
            yield
            y_conv = y_conv * _silu(proj(OFF_GC, CONV_WIDTH))
            mix_buf[sub, ATTN_WIDTH:ATTN_WIDTH + CONV_WIDTH] = y_conv.astype(bf16)
            yield

        def output():
            h = x + jnp.dot(mix_buf[sub, :], wout_ref[...], preferred_element_type=f32)
            out_ref[sub, :] = _rms_norm(h, final_gain)
            yield

        return {"q": attention_side(), "c": conv_side(), "o": output()}

    n_pass, n_units = tile // SUB_TILE, (SUB_TILE // BLOCK) * N_KV_HEADS
    streams = [sub_tile(k * SUB_TILE) for k in range(n_pass)]
    order = [(0, "q")] * 3 + [(0, "c")] * 3
    for k in range(n_pass):
        nxt = [(k + 1, "q")] * 3 + [(k + 1, "c")] * 3 if k + 1 < n_pass else []
        for u in range(n_units):
            order.append((k, "q"))
            if u == min(2, n_units - 1) - 1 and k > 0:
                order.append((k - 1, "o"))
            if nxt:
                order.append(nxt.pop(0))
        order += nxt
    order.append((n_pass - 1, "o"))
    for k, stream in order:
        next(streams[k][stream])

    u_buf[0:CONV_PAD, :] = u_buf[tile:tile + CONV_PAD, :]
    kz_buf[:, 0:BLOCK, :] = kz_buf[:, tile:tile + BLOCK, :]
    vx_buf[:, 0:BLOCK, :] = vx_buf[:, tile:tile + BLOCK, :]


def _pack_small_params(norm_g, final_g, conv_w):
    taps = jnp.pad(conv_w.astype(jnp.float32), ((0, 0), (0, D_MODEL - CONV_WIDTH)))
    rows = [norm_g.reshape(1, D_MODEL).astype(jnp.float32), final_g.reshape(1, D_MODEL).astype(jnp.float32), taps]
    pad = jnp.zeros((SMALL_ROWS - 2 - CONV_K, D_MODEL), jnp.float32)
    return jnp.concatenate(rows + [pad], axis=0)


@jax.jit
def kernel(x, norm_g, w_in, sinks, conv_w, w_out, final_g):
    batch, seq, d_model = x.shape
    tile = SEQ_TILE
    assert d_model == D_MODEL and seq % tile == 0 and tile % SUB_TILE == 0 and SUB_TILE % BLOCK == 0
    in_width = w_in.shape[1]
    cos_t, sin_lo, sin_hi = _rope_tables(seq)

    grid_spec = pltpu.PrefetchScalarGridSpec(
        num_scalar_prefetch=1,
        grid=(batch, seq // tile),
        in_specs=[
            pl.BlockSpec((None, tile, D_MODEL), lambda b, j, s: (b, j, 0)),
            pl.BlockSpec((SMALL_ROWS, D_MODEL), lambda b, j, s: (0, 0)),
            pl.BlockSpec((D_MODEL, in_width), lambda b, j, s: (0, 0)),
            pl.BlockSpec((D_MODEL, D_MODEL), lambda b, j, s: (0, 0)),
            pl.BlockSpec((tile, LANES), lambda b, j, s: (j, 0)),
            pl.BlockSpec((tile, LANES), lambda b, j, s: (j, 0)),
            pl.BlockSpec((tile, LANES), lambda b, j, s: (j, 0)),
        ],
        out_specs=pl.BlockSpec((None, tile, D_MODEL), lambda b, j, s: (b, j, 0)),
        scratch_shapes=[
            pltpu.VMEM((tile, 2 * ATTN_WIDTH), jnp.bfloat16),
            pltpu.VMEM((N_KV_HEADS, BLOCK + tile, LANES), jnp.bfloat16),
            pltpu.VMEM((N_KV_HEADS, BLOCK + tile, 2 * LANES), jnp.bfloat16),
            pltpu.VMEM((CONV_PAD + tile, CONV_WIDTH), jnp.float32),
            pltpu.VMEM((tile, D_MODEL), jnp.bfloat16),
        ],
    )
    return pl.pallas_call(
        _layer_kernel,
        grid_spec=grid_spec,
        out_shape=jax.ShapeDtypeStruct(x.shape, x.dtype),
        compiler_params=pltpu.CompilerParams(
            dimension_semantics=("arbitrary", "arbitrary"),
            vmem_limit_bytes=VMEM_LIMIT_BYTES),
        name="hybrid_layer",
    )(sinks.astype(jnp.float32), x, _pack_small_params(norm_g, final_g, conv_w),
      w_in.astype(jnp.bfloat16), w_out.astype(jnp.bfloat16),
      cos_t, sin_lo, sin_hi)
```

```python
import functools

import jax
import jax.numpy as jnp
import numpy as np
from jax import lax
from jax.experimental import pallas as pl
from jax.experimental.pallas import tpu as pltpu

D_MODEL = 1024
HEAD_DIM = 64
N_Q_HEADS = 8
N_KV_HEADS = 2
GROUP = N_Q_HEADS // N_KV_HEADS
ATTN_WIDTH = N_Q_HEADS * HEAD_DIM
KV_WIDTH = N_KV_HEADS * HEAD_DIM
BLOCK = 128
ROT_DIM = HEAD_DIM // 4
ROPE_THETA = 500000.0
CONV_WIDTH = D_MODEL - ATTN_WIDTH
CONV_K = 3
EPS = 1e-5
LOG2_E = 1.4426950408889634

LANES = 128
BF16_ROWS = 16
CONV_PAD = 8
SMALL_ROWS = 8
SEQ_TILE = 1024
SUB_TILE = 512
VMEM_LIMIT_BYTES = 56 * 1024 * 1024
TRACE_ORDER = ("Aq Aq Aq Ac Ac Ac "
               "Aq Bq Aq Bq Aq Bq Aq Bc Aq Bc Aq Bc Aq Aq "
               "Bq Ao Bq Bq Bq Bq Bq Bq Bq Bo")

OFF_Q = 0
OFF_KV = OFF_Q + ATTN_WIDTH
OFF_GA = OFF_KV + 2 * KV_WIDTH
OFF_B = OFF_GA + ATTN_WIDTH
OFF_C = OFF_B + CONV_WIDTH
OFF_H = OFF_C + CONV_WIDTH
OFF_GC = OFF_H + CONV_WIDTH


@functools.lru_cache(maxsize=None)
def _rope_tables(seq):
    half = ROT_DIM // 2
    inv_freq = ROPE_THETA ** (-np.arange(0, ROT_DIM, 2, dtype=np.float64) / ROT_DIM)
    ang = np.arange(seq, dtype=np.float64)[:, None] * inv_freq[None, :]
    cos, sin = np.cos(ang), np.sin(ang)
    ones = np.ones((seq, HEAD_DIM - ROT_DIM))
    zeros_h = np.zeros((seq, half))
    zeros_t = np.zeros((seq, HEAD_DIM - ROT_DIM))
    cos_t = np.concatenate([cos, cos, ones], axis=1)
    sin_lo = np.concatenate([-sin, zeros_h, zeros_t], axis=1)
    sin_hi = np.concatenate([zeros_h, sin, zeros_t], axis=1)
    rep = LANES // HEAD_DIM
    return tuple(np.tile(t, (1, rep)).astype(np.float32) for t in (cos_t, sin_lo, sin_hi))


def _rms_norm(x, g):
    ms = jnp.mean(x * x, axis=-1, keepdims=True)
    return x * lax.rsqrt(ms + EPS) * g


def _silu(x):
    h = 0.5 * x
    return h + h * jnp.tanh(h)


def _layer_kernel(sinks_ref, x_ref, par_ref, win_ref, wout_ref,
                  cos_ref, sinlo_ref, sinhi_ref, out_ref,
                  q_buf, kz_buf, vx_buf, u_buf, mix_buf):
    tile = x_ref.shape[0]
    j = pl.program_id(1)
    norm_gain, final_gain = par_ref[0:1, :], par_ref[1:2, :]
    taps = [par_ref[2 + k:3 + k, 0:CONV_WIDTH] for k in range(CONV_K)]
    f32, bf16 = jnp.float32, jnp.bfloat16

    @pl.when(j == 0)
    def _():
        kz_buf[:, 0:BLOCK, :] = jnp.zeros((N_KV_HEADS, BLOCK, LANES), bf16)
        vx_buf[:, 0:BLOCK, :] = jnp.ones((N_KV_HEADS, BLOCK, 2 * LANES), bf16)
        u_buf[0:CONV_PAD, :] = jnp.zeros((CONV_PAD, CONV_WIDTH), f32)

    low_half = lax.broadcasted_iota(jnp.int32, (SUB_TILE, LANES), 1) < HEAD_DIM
    qi = lax.broadcasted_iota(jnp.int32, (BLOCK, BLOCK), 0)
    kj = lax.broadcasted_iota(jnp.int32, (BLOCK, BLOCK), 1)
    neg_inf = jnp.float32(-jnp.inf)
    bias_cur = jnp.where(kj <= qi, 0.0, neg_inf)
    bias_prev_mid = jnp.where(kj > qi, 0.0, neg_inf)
    bias_prev_first = jnp.where((kj > qi) & (j > 0), 0.0, neg_inf)
    sink_slot = kj == 0
    bias_prev_sink = {}

    def prev_bias(head, first):
        if (head, first) not in bias_prev_sink:
            bias_prev_sink[head, first] = jnp.where(
                sink_slot, sinks_ref[head] * LOG2_E, bias_prev_first if first else bias_prev_mid)
        return bias_prev_sink[head, first]

    low_half_blk = kj < HEAD_DIM
    top_r = lax.broadcasted_iota(jnp.int32, (BF16_ROWS, 2 * LANES), 0)
    top_c = lax.broadcasted_iota(jnp.int32, (BF16_ROWS, 2 * LANES), 1)
    sink_value_mask = (top_r == 0) & ((top_c < HEAD_DIM) | (top_c >= 2 * LANES - HEAD_DIM))
    sink_key_mask = lax.broadcasted_iota(jnp.int32, (BF16_ROWS, LANES), 0) == 0
    nt_dims = (((1,), (1,)), ((), ()))
    scale = HEAD_DIM ** -0.5 * LOG2_E

    def sub_tile(r0):
        sub = slice(r0, r0 + SUB_TILE)
        x = x_ref[sub, :]
        xn = _rms_norm(x, norm_gain).astype(bf16)

        def proj(off, width):
            return jnp.dot(xn, win_ref[:, off:off + width], preferred_element_type=f32)

        def attention_side():
            cos_t, sin_lo, sin_hi = cos_ref[sub, :], sinlo_ref[sub, :], sinhi_ref[sub, :]

            def rope(t):
                return (t * cos_t + pltpu.roll(t, LANES - ROT_DIM // 2, 1) * sin_lo
                        + pltpu.roll(t, ROT_DIM // 2, 1) * sin_hi)

            q = proj(OFF_Q, ATTN_WIDTH)
            for c in range(ATTN_WIDTH // LANES):
                qc = rope(q[:, c * LANES:(c + 1) * LANES]) * scale
                q_buf[sub, c * LANES:(c + 1) * LANES] = qc.astype(bf16)
                q_buf[sub, ATTN_WIDTH + c * LANES:ATTN_WIDTH + (c + 1) * LANES] = (
                    pltpu.roll(qc, HEAD_DIM, 1).astype(bf16))
            yield

            kv = proj(OFF_KV, 2 * KV_WIDTH)
            new_rows = slice(BLOCK + r0, BLOCK + r0 + SUB_TILE)
            kr = rope(kv[:, 0:KV_WIDTH])
            kr_sw = pltpu.roll(kr, HEAD_DIM, 1)
            kz_buf[0, new_rows, :] = jnp.where(low_half, kr, 0.0).astype(bf16)
            kz_buf[1, new_rows, :] = jnp.where(low_half, kr_sw, 0.0).astype(bf16)
            vc = kv[:, KV_WIDTH:2 * KV_WIDTH]
            vc_sw = pltpu.roll(vc, HEAD_DIM, 1)
            vx_buf[0, new_rows, 0:LANES] = jnp.where(low_half, vc, 1.0).astype(bf16)
            vx_buf[0, new_rows, LANES:2 * LANES] = jnp.where(low_half, 1.0, vc_sw).astype(bf16)
            vx_buf[1, new_rows, 0:LANES] = jnp.where(low_half, vc_sw, 1.0).astype(bf16)
            vx_buf[1, new_rows, LANES:2 * LANES] = jnp.where(low_half, 1.0, vc).astype(bf16)
            yield

            g_attn = _silu(proj(OFF_GA, ATTN_WIDTH))
            yield

            for n in range(SUB_TILE // BLOCK):
                rows = slice(r0 + n * BLOCK, r0 + (n + 1) * BLOCK)
                keys = slice(r0 + n * BLOCK, r0 + (n + 2) * BLOCK)
                loc = slice(n * BLOCK, (n + 1) * BLOCK)
                first = r0 == 0 and n == 0
                for g in range(N_KV_HEADS):
                    base = g * GROUP * HEAD_DIM
                    q4 = jnp.concatenate(
                        [q_buf[rows, off:off + LANES]
                         for off in (base, base + LANES, ATTN_WIDTH + base, ATTN_WIDTH + base + LANES)], axis=0)
                    vx = vx_buf[g, keys, :]
                    vx = jnp.concatenate(
                        [jnp.where(sink_value_mask, 0.0, vx[0:BF16_ROWS]).astype(bf16), vx[BF16_ROWS:]], axis=0)
                    kz = kz_buf[g, keys, :]
                    kz = jnp.concatenate(
                        [jnp.where(sink_key_mask, 0.0, kz[0:BF16_ROWS]).astype(bf16), kz[BF16_ROWS:]], axis=0)
                    s = lax.dot_general(q4, kz, nt_dims, preferred_element_type=f32)
                    p = []
                    for slab, head in enumerate((0, 2, 1, 3)):
                        sc = s[slab * BLOCK:(slab + 1) * BLOCK]
                        s_prev = sc[:, 0:BLOCK] + prev_bias(g * GROUP + head, first)
                        s_cur = sc[:, BLOCK:2 * BLOCK] + bias_cur
                        m = jnp.max(jnp.maximum(s_prev, s_cur), axis=-1, keepdims=True)
                        p.append(jnp.concatenate(
                            [jnp.exp2(s_prev - m), jnp.exp2(s_cur - m)], axis=1).astype(bf16))
                    pv = jnp.dot(jnp.concatenate(p, axis=0), vx, preferred_element_type=f32)
                    for c in range(2):
                        lo = pv[c * BLOCK:(c + 1) * BLOCK]
                        hi = pv[(2 + c) * BLOCK:(3 + c) * BLOCK]
                        num = jnp.where(low_half_blk, lo[:, 0:LANES], hi[:, LANES:2 * LANES])
                        den = jnp.where(low_half_blk, lo[:, LANES:2 * LANES], hi[:, 0:LANES])
                        cols = slice(base + c * LANES, base + (c + 1) * LANES)
                        mix_buf[rows, cols] = (num / den * g_attn[loc, cols]).astype(bf16)
                    yield

        def conv_side():
            u = proj(OFF_C, CONV_WIDTH) * proj(OFF_H, CONV_WIDTH)
            u0 = CONV_PAD + r0
            u_buf[u0:u0 + SUB_TILE, :] = u
            conv = (taps[0] * u_buf[u0 - 2:u0 - 2 + SUB_TILE, :]
                    + taps[1] * u_buf[u0 - 1:u0 - 1 + SUB_TILE, :]
                    + taps[2] * u)
            yield
            y_conv = proj(OFF_B, CONV_WIDTH) * conv
            yield
            y_conv = y_conv * _silu(proj(OFF_GC, CONV_WIDTH))
            mix_buf[sub, ATTN_WIDTH:ATTN_WIDTH + CONV_WIDTH] = y_conv.astype(bf16)
            yield

        def output():
            h = x + jnp.dot(mix_buf[sub, :], wout_ref[...], preferred_element_type=f32)
            out_ref[sub, :] = _rms_norm(h, final_gain)
            yield

        return {"q": attention_side(), "c": conv_side(), "o": output()}

    assert tile == 2 * SUB_TILE and SUB_TILE == 4 * BLOCK
    streams = {"A": sub_tile(0), "B": sub_tile(SUB_TILE)}
    for token in TRACE_ORDER.split():
        next(streams[token[0]][token[1]])

    u_buf[0:CONV_PAD, :] = u_buf[tile:tile + CONV_PAD, :]
    kz_buf[:, 0:BLOCK, :] = kz_buf[:, tile:tile + BLOCK, :]
    vx_buf[:, 0:BLOCK, :] = vx_buf[:, tile:tile + BLOCK, :]


def _pack_small_params(norm_g, final_g, conv_w):
    taps = jnp.pad(conv_w.astype(jnp.float32), ((0, 0), (0, D_MODEL - CONV_WIDTH)))
    rows = [norm_g.reshape(1, D_MODEL).astype(jnp.float32), final_g.reshape(1, D_MODEL).astype(jnp.float32), taps]
    pad = jnp.zeros((SMALL_ROWS - 2 - CONV_K, D_MODEL), jnp.float32)
    return jnp.concatenate(rows + [pad], axis=0)


@jax.jit
def kernel(x, norm_g, w_in, sinks, conv_w, w_out, final_g):
    batch, seq, d_model = x.shape
    tile = SEQ_TILE
    assert d_model == D_MODEL and seq % tile == 0 and tile % SUB_TILE == 0 and SUB_TILE % BLOCK == 0
    in_width = w_in.shape[1]
    cos_t, sin_lo, sin_hi = _rope_tables(seq)

    grid_spec = pltpu.PrefetchScalarGridSpec(
        num_scalar_prefetch=1,
        grid=(batch, seq // tile),
        in_specs=[
            pl.BlockSpec((None, tile, D_MODEL), lambda b, j, s: (b, j, 0)),
            pl.BlockSpec((SMALL_ROWS, D_MODEL), lambda b, j, s: (0, 0)),
            pl.BlockSpec((D_MODEL, in_width), lambda b, j, s: (0, 0)),
            pl.BlockSpec((D_MODEL, D_MODEL), lambda b, j, s: (0, 0)),
            pl.BlockSpec((tile, LANES), lambda b, j, s: (j, 0)),
            pl.BlockSpec((tile, LANES), lambda b, j, s: (j, 0)),
            pl.BlockSpec((tile, LANES), lambda b, j, s: (j, 0)),
        ],
        out_specs=pl.BlockSpec((None, tile, D_MODEL), lambda b, j, s: (b, j, 0)),
        scratch_shapes=[
            pltpu.VMEM((tile, 2 * ATTN_WIDTH), jnp.bfloat16),
            pltpu.VMEM((N_KV_HEADS, BLOCK + tile, LANES), jnp.bfloat16),
            pltpu.VMEM((N_KV_HEADS, BLOCK + tile, 2 * LANES), jnp.bfloat16),
            pltpu.VMEM((CONV_PAD + tile, CONV_WIDTH), jnp.float32),
            pltpu.VMEM((tile, D_MODEL), jnp.bfloat16),
        ],
    )
    return pl.pallas_call(
        _layer_kernel,
        grid_spec=grid_spec,
        out_shape=jax.ShapeDtypeStruct(x.shape, x.dtype),
        compiler_params=pltpu.CompilerParams(
            dimension_semantics=("arbitrary", "arbitrary"),
            vmem_limit_bytes=VMEM_LIMIT_BYTES),
        name="hybrid_layer",
    )(sinks.astype(jnp.float32), x, _pack_small_params(norm_g, final_g, conv_w),
      w_in.astype(jnp.bfloat16), w_out.astype(jnp.bfloat16),
      cos_t, sin_lo, sin_hi)
```

```python
import functools

import jax
import jax.numpy as jnp
import numpy as np
from jax import lax
from jax.experimental import pallas as pl
from jax.experimental.pallas import tpu as pltpu

D_MODEL = 1024
HEAD_DIM = 64
N_Q_HEADS = 8
N_KV_HEADS = 2
GROUP = N_Q_HEADS // N_KV_HEADS
ATTN_WIDTH = N_Q_HEADS * HEAD_DIM
KV_WIDTH = N_KV_HEADS * HEAD_DIM
BLOCK = 128
ROT_DIM = HEAD_DIM // 4
ROPE_THETA = 500000.0
CONV_WIDTH = D_MODEL - ATTN_WIDTH
CONV_K = 3
EPS = 1e-5
LOG2_E = 1.4426950408889634

LANES = 128
BF16_ROWS = 16
CONV_PAD = 8
SMALL_ROWS = 8
SEQ_TILE = 1024
SUB_TILE = 1024
OUT_ROWS = 512
VMEM_LIMIT_BYTES = 56 * 1024 * 1024
TRACE_ORDER = ("Aq Aq Aq "
               "Aq Aq Ac Aq Aq Ac Aq Aq Ac Aq Aq "
               "Aq Aq Ao Aq Aq Aq Aq Aq Aq Ao")

OFF_Q = 0
OFF_KV = OFF_Q + ATTN_WIDTH
OFF_GA = OFF_KV + 2 * KV_WIDTH
OFF_B = OFF_GA + ATTN_WIDTH
OFF_C = OFF_B + CONV_WIDTH
OFF_H = OFF_C + CONV_WIDTH
OFF_GC = OFF_H + CONV_WIDTH


@functools.lru_cache(maxsize=None)
def _rope_tables(seq):
    half = ROT_DIM // 2
    inv_freq = ROPE_THETA ** (-np.arange(0, ROT_DIM, 2, dtype=np.float64) / ROT_DIM)
    ang = np.arange(seq, dtype=np.float64)[:, None] * inv_freq[None, :]
    cos, sin = np.cos(ang), np.sin(ang)
    ones = np.ones((seq, HEAD_DIM - ROT_DIM))
    zeros_h = np.zeros((seq, half))
    zeros_t = np.zeros((seq, HEAD_DIM - ROT_DIM))
    cos_t = np.concatenate([cos, cos, ones], axis=1)
    sin_lo = np.concatenate([-sin, zeros_h, zeros_t], axis=1)
    sin_hi = np.concatenate([zeros_h, sin, zeros_t], axis=1)
    rep = LANES // HEAD_DIM
    return tuple(np.tile(t, (1, rep)).astype(np.float32) for t in (cos_t, sin_lo, sin_hi))


def _rms_norm(x, g):
    ms = jnp.mean(x * x, axis=-1, keepdims=True)
    return x * lax.rsqrt(ms + EPS) * g


def _silu(x):
    h = 0.5 * x
    return h + h * jnp.tanh(h)


def _layer_kernel(sinks_ref, x_ref, par_ref, win_ref, wout_ref,
                  cos_ref, sinlo_ref, sinhi_ref, out_ref,
                  q_buf, kz_buf, vx_buf, u_buf, mix_buf):
    tile = x_ref.shape[0]
    j = pl.program_id(1)
    norm_gain, final_gain = par_ref[0:1, :], par_ref[1:2, :]
    taps = [par_ref[2 + k:3 + k, 0:CONV_WIDTH] for k in range(CONV_K)]
    f32, bf16 = jnp.float32, jnp.bfloat16

    @pl.when(j == 0)
    def _():
        kz_buf[:, 0:BLOCK, :] = jnp.zeros((N_KV_HEADS, BLOCK, LANES), bf16)
        vx_buf[:, 0:BLOCK, :] = jnp.ones((N_KV_HEADS, BLOCK, 2 * LANES), bf16)
        u_buf[0:CONV_PAD, :] = jnp.zeros((CONV_PAD, CONV_WIDTH), f32)

    low_half = lax.broadcasted_iota(jnp.int32, (SUB_TILE, LANES), 1) < HEAD_DIM
    qi = lax.broadcasted_iota(jnp.int32, (BLOCK, BLOCK), 0)
    kj = lax.broadcasted_iota(jnp.int32, (BLOCK, BLOCK), 1)
    neg_inf = jnp.float32(-jnp.inf)
    bias_cur = jnp.where(kj <= qi, 0.0, neg_inf)
    bias_prev_mid = jnp.where(kj > qi, 0.0, neg_inf)
    bias_prev_first = jnp.where((kj > qi) & (j > 0), 0.0, neg_inf)
    sink_slot = kj == 0
    bias_prev_sink = {}

    def prev_bias(head, first):
        if (head, first) not in bias_prev_sink:
            bias_prev_sink[head, first] = jnp.where(
                sink_slot, sinks_ref[head] * LOG2_E, bias_prev_first if first else bias_prev_mid)
        return bias_prev_sink[head, first]

    low_half_blk = kj < HEAD_DIM
    top_r = lax.broadcasted_iota(jnp.int32, (BF16_ROWS, 2 * LANES), 0)
    top_c = lax.broadcasted_iota(jnp.int32, (BF16_ROWS, 2 * LANES), 1)
    sink_value_mask = (top_r == 0) & ((top_c < HEAD_DIM) | (top_c >= 2 * LANES - HEAD_DIM))
    sink_key_mask = lax.broadcasted_iota(jnp.int32, (BF16_ROWS, LANES), 0) == 0
    nt_dims = (((1,), (1,)), ((), ()))
    scale = HEAD_DIM ** -0.5 * LOG2_E

    def sub_tile(r0):
        sub = slice(r0, r0 + SUB_TILE)
        x = x_ref[sub, :]
        xn = _rms_norm(x, norm_gain).astype(bf16)

        def proj(off, width):
            return jnp.dot(xn, win_ref[:, off:off + width], preferred_element_type=f32)

        def attention_side():
            cos_t, sin_lo, sin_hi = cos_ref[sub, :], sinlo_ref[sub, :], sinhi_ref[sub, :]

            def rope(t):
                return (t * cos_t + pltpu.roll(t, LANES - ROT_DIM // 2, 1) * sin_lo
                        + pltpu.roll(t, ROT_DIM // 2, 1) * sin_hi)

            q = proj(OFF_Q, ATTN_WIDTH)
            for c in range(ATTN_WIDTH // LANES):
                qc = rope(q[:, c * LANES:(c + 1) * LANES]) * scale
                q_buf[sub, c * LANES:(c + 1) * LANES] = qc.astype(bf16)
                q_buf[sub, ATTN_WIDTH + c * LANES:ATTN_WIDTH + (c + 1) * LANES] = (
                    pltpu.roll(qc, HEAD_DIM, 1).astype(bf16))
            yield

            kv = proj(OFF_KV, 2 * KV_WIDTH)
            new_rows = slice(BLOCK + r0, BLOCK + r0 + SUB_TILE)
            kr = rope(kv[:, 0:KV_WIDTH])
            kr_sw = pltpu.roll(kr, HEAD_DIM, 1)
            kz_buf[0, new_rows, :] = jnp.where(low_half, kr, 0.0).astype(bf16)
            kz_buf[1, new_rows, :] = jnp.where(low_half, kr_sw, 0.0).astype(bf16)
            vc = kv[:, KV_WIDTH:2 * KV_WIDTH]
            vc_sw = pltpu.roll(vc, HEAD_DIM, 1)
            vx_buf[0, new_rows, 0:LANES] = jnp.where(low_half, vc, 1.0).astype(bf16)
            vx_buf[0, new_rows, LANES:2 * LANES] = jnp.where(low_half, 1.0, vc_sw).astype(bf16)
            vx_buf[1, new_rows, 0:LANES] = jnp.where(low_half, vc_sw, 1.0).astype(bf16)
            vx_buf[1, new_rows, LANES:2 * LANES] = jnp.where(low_half, 1.0, vc).astype(bf16)
            yield

            g_attn = _silu(proj(OFF_GA, ATTN_WIDTH))
            yield

            for n in range(SUB_TILE // BLOCK):
                rows = slice(r0 + n * BLOCK, r0 + (n + 1) * BLOCK)
                keys = slice(r0 + n * BLOCK, r0 + (n + 2) * BLOCK)
                loc = slice(n * BLOCK, (n + 1) * BLOCK)
                first = r0 == 0 and n == 0
                for g in range(N_KV_HEADS):
                    base = g * GROUP * HEAD_DIM
                    q4 = jnp.concatenate(
                        [q_buf[rows, off:off + LANES]
                         for off in (base, base + LANES, ATTN_WIDTH + base, ATTN_WIDTH + base + LANES)], axis=0)
                    vx = vx_buf[g, keys, :]
                    vx = jnp.concatenate(
                        [jnp.where(sink_value_mask, 0.0, vx[0:BF16_ROWS]).astype(bf16), vx[BF16_ROWS:]], axis=0)
                    kz = kz_buf[g, keys, :]
                    kz = jnp.concatenate(
                        [jnp.where(sink_key_mask, 0.0, kz[0:BF16_ROWS]).astype(bf16), kz[BF16_ROWS:]], axis=0)
                    s = lax.dot_general(q4, kz, nt_dims, preferred_element_type=f32)
                    p = []
                    for slab, head in enumerate((0, 2, 1, 3)):
                        sc = s[slab * BLOCK:(slab + 1) * BLOCK]
                        s_prev = sc[:, 0:BLOCK] + prev_bias(g * GROUP + head, first)
                        s_cur = sc[:, BLOCK:2 * BLOCK] + bias_cur
                        m = jnp.max(jnp.maximum(s_prev, s_cur), axis=-1, keepdims=True)
                        p.append(jnp.concatenate(
                            [jnp.exp2(s_prev - m), jnp.exp2(s_cur - m)], axis=1).astype(bf16))
                    pv = jnp.dot(jnp.concatenate(p, axis=0), vx, preferred_element_type=f32)
                    for c in range(2):
                        lo = pv[c * BLOCK:(c + 1) * BLOCK]
                        hi = pv[(2 + c) * BLOCK:(3 + c) * BLOCK]
                        num = jnp.where(low_half_blk, lo[:, 0:LANES], hi[:, LANES:2 * LANES])
                        den = jnp.where(low_half_blk, lo[:, LANES:2 * LANES], hi[:, 0:LANES])
                        cols = slice(base + c * LANES, base + (c + 1) * LANES)
                        mix_buf[rows, cols] = (num / den * g_attn[loc, cols]).astype(bf16)
                    yield

        def conv_side():
            u = proj(OFF_C, CONV_WIDTH) * proj(OFF_H, CONV_WIDTH)
            u0 = CONV_PAD + r0
            u_buf[u0:u0 + SUB_TILE, :] = u
            conv = (taps[0] * u_buf[u0 - 2:u0 - 2 + SUB_TILE, :]
                    + taps[1] * u_buf[u0 - 1:u0 - 1 + SUB_TILE, :]
                    + taps[2] * u)
            yield
            y_conv = proj(OFF_B, CONV_WIDTH) * conv
            yield
            y_conv = y_conv * _silu(proj(OFF_GC, CONV_WIDTH))
            mix_buf[sub, ATTN_WIDTH:ATTN_WIDTH + CONV_WIDTH] = y_conv.astype(bf16)
            yield

        def output():
            for h0 in range(0, SUB_TILE, OUT_ROWS):
                part = slice(r0 + h0, r0 + h0 + OUT_ROWS)
                h = x[h0:h0 + OUT_ROWS] + jnp.dot(mix_buf[part, :], wout_ref[...], preferred_element_type=f32)
                out_ref[part, :] = _rms_norm(h, final_gain)
                yield

        return {"q": attention_side(), "c": conv_side(), "o": output()}

    assert tile == SUB_TILE
    streams = {"A": sub_tile(0)}
    for token in TRACE_ORDER.split():
        next(streams[token[0]][token[1]])

    u_buf[0:CONV_PAD, :] = u_buf[tile:tile + CONV_PAD, :]
    kz_buf[:, 0:BLOCK, :] = kz_buf[:, tile:tile + BLOCK, :]
    vx_buf[:, 0:BLOCK, :] = vx_buf[:, tile:tile + BLOCK, :]


def _pack_small_params(norm_g, final_g, conv_w):
    taps = jnp.pad(conv_w.astype(jnp.float32), ((0, 0), (0, D_MODEL - CONV_WIDTH)))
    rows = [norm_g.reshape(1, D_MODEL).astype(jnp.float32), final_g.reshape(1, D_MODEL).astype(jnp.float32), taps]
    pad = jnp.zeros((SMALL_ROWS - 2 - CONV_K, D_MODEL), jnp.float32)
    return jnp.concatenate(rows + [pad], axis=0)


@jax.jit
def kernel(x, norm_g, w_in, sinks, conv_w, w_out, final_g):
    batch, seq, d_model = x.shape
    tile = SEQ_TILE
    assert d_model == D_MODEL and seq % tile == 0 and tile % SUB_TILE == 0 and SUB_TILE % BLOCK == 0
    in_width = w_in.shape[1]
    cos_t, sin_lo, sin_hi = _rope_tables(seq)

    grid_spec = pltpu.PrefetchScalarGridSpec(
        num_scalar_prefetch=1,
        grid=(batch, seq // tile),
        in_specs=[
            pl.BlockSpec((None, tile, D_MODEL), lambda b, j, s: (b, j, 0)),
            pl.BlockSpec((SMALL_ROWS, D_MODEL), lambda b, j, s: (0, 0)),
            pl.BlockSpec((D_MODEL, in_width), lambda b, j, s: (0, 0)),
            pl.BlockSpec((D_MODEL, D_MODEL), lambda b, j, s: (0, 0)),
            pl.BlockSpec((tile, LANES), lambda b, j, s: (j, 0)),
            pl.BlockSpec((tile, LANES), lambda b, j, s: (j, 0)),
            pl.BlockSpec((tile, LANES), lambda b, j, s: (j, 0)),
        ],
        out_specs=pl.BlockSpec((None, tile, D_MODEL), lambda b, j, s: (b, j, 0)),
        scratch_shapes=[
            pltpu.VMEM((tile, 2 * ATTN_WIDTH), jnp.bfloat16),
            pltpu.VMEM((N_KV_HEADS, BLOCK + tile, LANES), jnp.bfloat16),
            pltpu.VMEM((N_KV_HEADS, BLOCK + tile, 2 * LANES), jnp.bfloat16),
            pltpu.VMEM((CONV_PAD + tile, CONV_WIDTH), jnp.float32),
            pltpu.VMEM((tile, D_MODEL), jnp.bfloat16),
        ],
    )
    return pl.pallas_call(
        _layer_kernel,
        grid_spec=grid_spec,
        out_shape=jax.ShapeDtypeStruct(x.shape, x.dtype),
        compiler_params=pltpu.CompilerParams(
            dimension_semantics=("arbitrary", "arbitrary"),
            vmem_limit_bytes=VMEM_LIMIT_BYTES),
        name="hybrid_layer",
    )(sinks.astype(jnp.float32), x, _pack_small_params(norm_g, final_g, conv_w),
      w_in.astype(jnp.bfloat16), w_out.astype(jnp.bfloat16),
      cos_t, sin_lo, sin_hi)
```

```python
import functools

import jax
import jax.numpy as jnp
import numpy as np
from jax import lax
from jax.experimental import pallas as pl
from jax.experimental.pallas import tpu as pltpu

D_MODEL = 1024
HEAD_DIM = 64
N_Q_HEADS = 8
N_KV_HEADS = 2
GROUP = N_Q_HEADS // N_KV_HEADS
ATTN_WIDTH = N_Q_HEADS * HEAD_DIM
KV_WIDTH = N_KV_HEADS * HEAD_DIM
BLOCK = 128
ROT_DIM = HEAD_DIM // 4
ROPE_THETA = 500000.0
CONV_WIDTH = D_MODEL - ATTN_WIDTH
CONV_K = 3
EPS = 1e-5
LOG2_E = 1.4426950408889634

LANES = 128
BF16_ROWS = 16
CONV_PAD = 8
SMALL_ROWS = 8
SEQ_TILE = 1024
SUB_TILE = 512
VMEM_LIMIT_BYTES = 56 * 1024 * 1024
TRACE_ORDER = ("Aq Aq Aq Ac Ac Ac "
               "Aq Bq Aq Bq Aq Bq Aq Bc Aq Bc Aq Bc Aq Aq "
               "Bq Bq Ao Bq Bq Bq Bq Bq Bq Bo")

OFF_Q = 0
OFF_KV = OFF_Q + ATTN_WIDTH
OFF_GA = OFF_KV + 2 * KV_WIDTH
OFF_B = OFF_GA + ATTN_WIDTH
OFF_C = OFF_B + CONV_WIDTH
OFF_H = OFF_C + CONV_WIDTH
OFF_GC = OFF_H + CONV_WIDTH


@functools.lru_cache(maxsize=None)
def _rope_tables(seq):
    half = ROT_DIM // 2
    inv_freq = ROPE_THETA ** (-np.arange(0, ROT_DIM, 2, dtype=np.float64) / ROT_DIM)
    ang = np.arange(seq, dtype=np.float64)[:, None] * inv_freq[None, :]
    cos, sin = np.cos(ang), np.sin(ang)
    ones = np.ones((seq, HEAD_DIM - ROT_DIM))
    zeros_h = np.zeros((seq, half))
    zeros_t = np.zeros((seq, HEAD_DIM - ROT_DIM))
    cos_t = np.concatenate([cos, cos, ones], axis=1)
    sin_lo = np.concatenate([-sin, zeros_h, zeros_t], axis=1)
    sin_hi = np.concatenate([zeros_h, sin, zeros_t], axis=1)
    rep = LANES // HEAD_DIM
    return tuple(np.tile(t, (1, rep)).astype(np.float32) for t in (cos_t, sin_lo, sin_hi))


def _rms_norm(x, g):
    ms = jnp.mean(x * x, axis=-1, keepdims=True)
    return x * lax.rsqrt(ms + EPS) * g


def _silu(x):
    h = 0.5 * x
    return h + h * jnp.tanh(h)


def _layer_kernel(sinks_ref, x_ref, par_ref, win_ref, wout_ref,
                  cos_ref, sinlo_ref, sinhi_ref, out_ref,
                  q_buf, kz_buf, vx_buf, u_buf, mix_buf):
    tile = x_ref.shape[0]
    j = pl.program_id(1)
    norm_gain, final_gain = par_ref[0:1, :], par_ref[1:2, :]
    taps = [par_ref[2 + k:3 + k, 0:CONV_WIDTH] for k in range(CONV_K)]
    f32, bf16 = jnp.float32, jnp.bfloat16

    @pl.when(j == 0)
    def _():
        kz_buf[:, 0:BLOCK, :] = jnp.zeros((N_KV_HEADS, BLOCK, LANES), bf16)
        vx_buf[:, 0:BLOCK, :] = jnp.ones((N_KV_HEADS, BLOCK, 2 * LANES), bf16)
        u_buf[0:CONV_PAD, :] = jnp.zeros((CONV_PAD, CONV_WIDTH), f32)

    low_half = lax.broadcasted_iota(jnp.int32, (SUB_TILE, LANES), 1) < HEAD_DIM
    qi = lax.broadcasted_iota(jnp.int32, (BLOCK, BLOCK), 0)
    kj = lax.broadcasted_iota(jnp.int32, (BLOCK, BLOCK), 1)
    neg_inf = jnp.float32(-jnp.inf)
    bias_cur = jnp.where(kj <= qi, 0.0, neg_inf)
    bias_prev_mid = jnp.where(kj > qi, 0.0, neg_inf)
    bias_prev_first = jnp.where((kj > qi) & (j > 0), 0.0, neg_inf)
    sink_slot = kj == 0
    bias_prev_sink = {}

    def prev_bias(head, first):
        if (head, first) not in bias_prev_sink:
            bias_prev_sink[head, first] = jnp.where(
                sink_slot, sinks_ref[head] * LOG2_E, bias_prev_first if first else bias_prev_mid)
        return bias_prev_sink[head, first]

    low_half_blk = kj < HEAD_DIM
    top_r = lax.broadcasted_iota(jnp.int32, (BF16_ROWS, 2 * LANES), 0)
    top_c = lax.broadcasted_iota(jnp.int32, (BF16_ROWS, 2 * LANES), 1)
    sink_value_mask = (top_r == 0) & ((top_c < HEAD_DIM) | (top_c >= 2 * LANES - HEAD_DIM))
    sink_key_mask = lax.broadcasted_iota(jnp.int32, (BF16_ROWS, LANES), 0) == 0
    nt_dims = (((1,), (1,)), ((), ()))
    scale = HEAD_DIM ** -0.5 * LOG2_E

    def sub_tile(r0):
        sub = slice(r0, r0 + SUB_TILE)
        x = x_ref[sub, :]
        xn = _rms_norm(x, norm_gain).astype(bf16)

        def proj(off, width):
            return jnp.dot(xn, win_ref[:, off:off + width], preferred_element_type=f32)

        def attention_side():
            pos = pl.ds(pl.multiple_of(j * tile + r0, SUB_TILE), SUB_TILE)
            cos_t, sin_lo, sin_hi = cos_ref[pos, :], sinlo_ref[pos, :], sinhi_ref[pos, :]

            def rope(t):
                return (t * cos_t + pltpu.roll(t, LANES - ROT_DIM // 2, 1) * sin_lo
                        + pltpu.roll(t, ROT_DIM // 2, 1) * sin_hi)

            q = proj(OFF_Q, ATTN_WIDTH)
            for c in range(ATTN_WIDTH // LANES):
                qc = rope(q[:, c * LANES:(c + 1) * LANES]) * scale
                q_buf[sub, c * LANES:(c + 1) * LANES] = qc.astype(bf16)
                q_buf[sub, ATTN_WIDTH + c * LANES:ATTN_WIDTH + (c + 1) * LANES] = (
                    pltpu.roll(qc, HEAD_DIM, 1).astype(bf16))
            yield

            kv = proj(OFF_KV, 2 * KV_WIDTH)
            new_rows = slice(BLOCK + r0, BLOCK + r0 + SUB_TILE)
            kr = rope(kv[:, 0:KV_WIDTH])
            kr_sw = pltpu.roll(kr, HEAD_DIM, 1)
            kz_buf[0, new_rows, :] = jnp.where(low_half, kr, 0.0).astype(bf16)
            kz_buf[1, new_rows, :] = jnp.where(low_half, kr_sw, 0.0).astype(bf16)
            vc = kv[:, KV_WIDTH:2 * KV_WIDTH]
            vc_sw = pltpu.roll(vc, HEAD_DIM, 1)
            vx_buf[0, new_rows, 0:LANES] = jnp.where(low_half, vc, 1.0).astype(bf16)
            vx_buf[0, new_rows, LANES:2 * LANES] = jnp.where(low_half, 1.0, vc_sw).astype(bf16)
            vx_buf[1, new_rows, 0:LANES] = jnp.where(low_half, vc_sw, 1.0).astype(bf16)
            vx_buf[1, new_rows, LANES:2 * LANES] = jnp.where(low_half, 1.0, vc).astype(bf16)
            yield

            g_attn = _silu(proj(OFF_GA, ATTN_WIDTH))
            yield

            for n in range(SUB_TILE // BLOCK):
                rows = slice(r0 + n * BLOCK, r0 + (n + 1) * BLOCK)
                keys = slice(r0 + n * BLOCK, r0 + (n + 2) * BLOCK)
                loc = slice(n * BLOCK, (n + 1) * BLOCK)
                first = r0 == 0 and n == 0
                for g in range(N_KV_HEADS):
                    base = g * GROUP * HEAD_DIM
                    q4 = jnp.concatenate(
                        [q_buf[rows, off:off + LANES]
                         for off in (base, base + LANES, ATTN_WIDTH + base, ATTN_WIDTH + base + LANES)], axis=0)
                    vx = vx_buf[g, keys, :]
                    vx = jnp.concatenate(
                        [jnp.where(sink_value_mask, 0.0, vx[0:BF16_ROWS]).astype(bf16), vx[BF16_ROWS:]], axis=0)
                    kz = kz_buf[g, keys, :]
                    kz = jnp.concatenate(
                        [jnp.where(sink_key_mask, 0.0, kz[0:BF16_ROWS]).astype(bf16), kz[BF16_ROWS:]], axis=0)
                    s = lax.dot_general(q4, kz, nt_dims, preferred_element_type=f32)
                    p = []
                    for slab, head in enumerate((0, 2, 1, 3)):
                        sc = s[slab * BLOCK:(slab + 1) * BLOCK]
                        s_prev = sc[:, 0:BLOCK] + prev_bias(g * GROUP + head, first)
                        s_cur = sc[:, BLOCK:2 * BLOCK] + bias_cur
                        m = jnp.max(jnp.maximum(s_prev, s_cur), axis=-1, keepdims=True)
                        p.append(jnp.concatenate(
                            [jnp.exp2(s_prev - m), jnp.exp2(s_cur - m)], axis=1).astype(bf16))
                    pv = jnp.dot(jnp.concatenate(p, axis=0), vx, preferred_element_type=f32)
                    for c in range(2):
                        lo = pv[c * BLOCK:(c + 1) * BLOCK]
                        hi = pv[(2 + c) * BLOCK:(3 + c) * BLOCK]
                        num = jnp.where(low_half_blk, lo[:, 0:LANES], hi[:, LANES:2 * LANES])
                        den = jnp.where(low_half_blk, lo[:, LANES:2 * LANES], hi[:, 0:LANES])
                        cols = slice(base + c * LANES, base + (c + 1) * LANES)
                        mix_buf[rows, cols] = (num / den * g_attn[loc, cols]).astype(bf16)
                    yield

        def conv_side():
            u = proj(OFF_C, CONV_WIDTH) * proj(OFF_H, CONV_WIDTH)
            u0 = CONV_PAD + r0
            u_buf[u0:u0 + SUB_TILE, :] = u
            conv = (taps[0] * u_buf[u0 - 2:u0 - 2 + SUB_TILE, :]
                    + taps[1] * u_buf[u0 - 1:u0 - 1 + SUB_TILE, :]
                    + taps[2] * u)
            yield
            y_conv = proj(OFF_B, CONV_WIDTH) * conv
            yield
            y_conv = y_conv * _silu(proj(OFF_GC, CONV_WIDTH))
            mix_buf[sub, ATTN_WIDTH:ATTN_WIDTH + CONV_WIDTH] = y_conv.astype(bf16)
            yield

        def output():
            h = x + jnp.dot(mix_buf[sub, :], wout_ref[...], preferred_element_type=f32)
            out_ref[sub, :] = _rms_norm(h, final_gain)
            yield

        return {"q": attention_side(), "c": conv_side(), "o": output()}

    assert tile == 2 * SUB_TILE and SUB_TILE == 4 * BLOCK
    streams = {"A": sub_tile(0), "B": sub_tile(SUB_TILE)}
    for token in TRACE_ORDER.split():
        next(streams[token[0]][token[1]])

    u_buf[0:CONV_PAD, :] = u_buf[tile:tile + CONV_PAD, :]
    kz_buf[:, 0:BLOCK, :] = kz_buf[:, tile:tile + BLOCK, :]
    vx_buf[:, 0:BLOCK, :] = vx_buf[:, tile:tile + BLOCK, :]


def _pack_small_params(norm_g, final_g, conv_w):
    taps = jnp.pad(conv_w.astype(jnp.float32), ((0, 0), (0, D_MODEL - CONV_WIDTH)))
    rows = [norm_g.reshape(1, D_MODEL).astype(jnp.float32), final_g.reshape(1, D_MODEL).astype(jnp.float32), taps]
    pad = jnp.zeros((SMALL_ROWS - 2 - CONV_K, D_MODEL), jnp.float32)
    return jnp.concatenate(rows + [pad], axis=0)


@jax.jit
def kernel(x, norm_g, w_in, sinks, conv_w, w_out, final_g):
    batch, seq, d_model = x.shape
    tile = SEQ_TILE
    assert d_model == D_MODEL and seq % tile == 0 and tile % SUB_TILE == 0 and SUB_TILE % BLOCK == 0
    in_width = w_in.shape[1]
    cos_t, sin_lo, sin_hi = _rope_tables(seq)

    grid_spec = pltpu.PrefetchScalarGridSpec(
        num_scalar_prefetch=1,
        grid=(batch, seq // tile),
        in_specs=[
            pl.BlockSpec((None, tile, D_MODEL), lambda b, j, s: (b, j, 0)),
            pl.BlockSpec((SMALL_ROWS, D_MODEL), lambda b, j, s: (0, 0)),
            pl.BlockSpec((D_MODEL, in_width), lambda b, j, s: (0, 0)),
            pl.BlockSpec((D_MODEL, D_MODEL), lambda b, j, s: (0, 0)),
            pl.BlockSpec((seq, LANES), lambda b, j, s: (0, 0)),
            pl.BlockSpec((seq, LANES), lambda b, j, s: (0, 0)),
            pl.BlockSpec((seq, LANES), lambda b, j, s: (0, 0)),
        ],
        out_specs=pl.BlockSpec((None, tile, D_MODEL), lambda b, j, s: (b, j, 0)),
        scratch_shapes=[
            pltpu.VMEM((tile, 2 * ATTN_WIDTH), jnp.bfloat16),
            pltpu.VMEM((N_KV_HEADS, BLOCK + tile, LANES), jnp.bfloat16),
            pltpu.VMEM((N_KV_HEADS, BLOCK + tile, 2 * LANES), jnp.bfloat16),
            pltpu.VMEM((CONV_PAD + tile, CONV_WIDTH), jnp.float32),
            pltpu.VMEM((tile, D_MODEL), jnp.bfloat16),
        ],
    )
    return pl.pallas_call(
        _layer_kernel,
        grid_spec=grid_spec,
        out_shape=jax.ShapeDtypeStruct(x.shape, x.dtype),
        compiler_params=pltpu.CompilerParams(
            dimension_semantics=("arbitrary", "arbitrary"),
            vmem_limit_bytes=VMEM_LIMIT_BYTES),
        name="hybrid_layer",
    )(sinks.astype(jnp.float32), x, _pack_small_params(norm_g, final_g, conv_w),
      w_in.astype(jnp.bfloat16), w_out.astype(jnp.bfloat16),
      cos_t, sin_lo, sin_hi)
```

```python
import functools

import jax
import jax.numpy as jnp
import numpy as np
from jax import lax
from jax.experimental import pallas as pl
from jax.experimental.pallas import tpu as pltpu

D_MODEL = 1024
HEAD_DIM = 64
N_Q_HEADS = 8
N_KV_HEADS = 2
GROUP = N_Q_HEADS // N_KV_HEADS
ATTN_WIDTH = N_Q_HEADS * HEAD_DIM
KV_WIDTH = N_KV_HEADS * HEAD_DIM
BLOCK = 128
ROT_DIM = HEAD_DIM // 4
ROPE_THETA = 500000.0
CONV_WIDTH = D_MODEL - ATTN_WIDTH
CONV_K = 3
EPS = 1e-5
LOG2_E = 1.4426950408889634

LANES = 128
BF16_ROWS = 16
CONV_PAD = 8
SMALL_ROWS = 8
SEQ_TILE = 1024
SUB_TILE = 512
VMEM_LIMIT_BYTES = 56 * 1024 * 1024
TRACE_ORDER = ("Aq Aq Aq Ac Ac Ac "
               "Aq Bq Aq Bq Aq Bq Aq Bc Aq Bc Aq Aq Bc Aq "
               "Bq Bq Ao Bq Bq Bq Bq Bq Bq Bo")

OFF_Q = 0
OFF_KV = OFF_Q + ATTN_WIDTH
OFF_GA = OFF_KV + 2 * KV_WIDTH
OFF_B = OFF_GA + ATTN_WIDTH
OFF_C = OFF_B + CONV_WIDTH
OFF_H = OFF_C + CONV_WIDTH
OFF_GC = OFF_H + CONV_WIDTH


@functools.lru_cache(maxsize=None)
def _rope_tables(seq):
    half = ROT_DIM // 2
    inv_freq = ROPE_THETA ** (-np.arange(0, ROT_DIM, 2, dtype=np.float64) / ROT_DIM)
    ang = np.arange(seq, dtype=np.float64)[:, None] * inv_freq[None, :]
    cos, sin = np.cos(ang), np.sin(ang)
    ones = np.ones((seq, HEAD_DIM - ROT_DIM))
    zeros_h = np.zeros((seq, half))
    zeros_t = np.zeros((seq, HEAD_DIM - ROT_DIM))
    cos_t = np.concatenate([cos, cos, ones], axis=1)
    sin_lo = np.concatenate([-sin, zeros_h, zeros_t], axis=1)
    sin_hi = np.concatenate([zeros_h, sin, zeros_t], axis=1)
    rep = LANES // HEAD_DIM
    return tuple(np.tile(t, (1, rep)).astype(np.float32) for t in (cos_t, sin_lo, sin_hi))


def _rms_norm(x, g):
    ms = jnp.mean(x * x, axis=-1, keepdims=True)
    return x * lax.rsqrt(ms + EPS) * g


def _silu(x):
    h = 0.5 * x
    return h + h * jnp.tanh(h)


def _layer_kernel(sinks_ref, x_ref, par_ref, win_ref, wout_ref,
                  cos_ref, sinlo_ref, sinhi_ref, out_ref,
                  q_buf, kz_buf, vx_buf, u_buf, mix_buf):
    tile = x_ref.shape[0]
    j = pl.program_id(1)
    norm_gain, final_gain = par_ref[0:1, :], par_ref[1:2, :]
    taps = [par_ref[2 + k:3 + k, 0:CONV_WIDTH] for k in range(CONV_K)]
    f32, bf16 = jnp.float32, jnp.bfloat16

    @pl.when(j == 0)
    def _():
        kz_buf[:, 0:BLOCK, :] = jnp.zeros((N_KV_HEADS, BLOCK, LANES), bf16)
        vx_buf[:, 0:BLOCK, :] = jnp.ones((N_KV_HEADS, BLOCK, 2 * LANES), bf16)
        u_buf[0:CONV_PAD, :] = jnp.zeros((CONV_PAD, CONV_WIDTH), f32)

    low_half = lax.broadcasted_iota(jnp.int32, (SUB_TILE, LANES), 1) < HEAD_DIM
    qi = lax.broadcasted_iota(jnp.int32, (BLOCK, BLOCK), 0)
    kj = lax.broadcasted_iota(jnp.int32, (BLOCK, BLOCK), 1)
    neg_inf = jnp.float32(-jnp.inf)
    bias_cur = jnp.where(kj <= qi, 0.0, neg_inf)
    bias_prev_mid = jnp.where(kj > qi, 0.0, neg_inf)
    bias_prev_first = jnp.where((kj > qi) & (j > 0), 0.0, neg_inf)
    sink_slot = kj == 0
    bias_prev_sink = {}

    def prev_bias(head, first):
        if (head, first) not in bias_prev_sink:
            bias_prev_sink[head, first] = jnp.where(
                sink_slot, sinks_ref[head] * LOG2_E, bias_prev_first if first else bias_prev_mid)
        return bias_prev_sink[head, first]

    low_half_blk = kj < HEAD_DIM
    top_r = lax.broadcasted_iota(jnp.int32, (BF16_ROWS, 2 * LANES), 0)
    top_c = lax.broadcasted_iota(jnp.int32, (BF16_ROWS, 2 * LANES), 1)
    sink_value_mask = (top_r == 0) & ((top_c < HEAD_DIM) | (top_c >= 2 * LANES - HEAD_DIM))
    sink_key_mask = lax.broadcasted_iota(jnp.int32, (BF16_ROWS, LANES), 0) == 0
    nt_dims = (((1,), (1,)), ((), ()))
    scale = HEAD_DIM ** -0.5 * LOG2_E

    def sub_tile(r0):
        sub = slice(r0, r0 + SUB_TILE)
        x = x_ref[sub, :]
        xn = _rms_norm(x, norm_gain).astype(bf16)

        def proj(off, width):
            return jnp.dot(xn, win_ref[:, off:off + width], preferred_element_type=f32)

        def attention_side():
            cos_t, sin_lo, sin_hi = cos_ref[sub, :], sinlo_ref[sub, :], sinhi_ref[sub, :]

            def rope(t):
                return (t * cos_t + pltpu.roll(t, LANES - ROT_DIM // 2, 1) * sin_lo
                        + pltpu.roll(t, ROT_DIM // 2, 1) * sin_hi)

            q = proj(OFF_Q, ATTN_WIDTH)
            for c in range(ATTN_WIDTH // LANES):
                qc = rope(q[:, c * LANES:(c + 1) * LANES]) * scale
                q_buf[sub, c * LANES:(c + 1) * LANES] = qc.astype(bf16)
                q_buf[sub, ATTN_WIDTH + c * LANES:ATTN_WIDTH + (c + 1) * LANES] = (
                    pltpu.roll(qc, HEAD_DIM, 1).astype(bf16))
            yield

            kv = proj(OFF_KV, 2 * KV_WIDTH)
            new_rows = slice(BLOCK + r0, BLOCK + r0 + SUB_TILE)
            kr = rope(kv[:, 0:KV_WIDTH])
            kr_sw = pltpu.roll(kr, HEAD_DIM, 1)
            kz_buf[0, new_rows, :] = jnp.where(low_half, kr, 0.0).astype(bf16)
            kz_buf[1, new_rows, :] = jnp.where(low_half, kr_sw, 0.0).astype(bf16)
            vc = kv[:, KV_WIDTH:2 * KV_WIDTH]
            vc_sw = pltpu.roll(vc, HEAD_DIM, 1)
            vx_buf[0, new_rows, 0:LANES] = jnp.where(low_half, vc, 1.0).astype(bf16)
            vx_buf[0, new_rows, LANES:2 * LANES] = jnp.where(low_half, 1.0, vc_sw).astype(bf16)
            vx_buf[1, new_rows, 0:LANES] = jnp.where(low_half, vc_sw, 1.0).astype(bf16)
            vx_buf[1, new_rows, LANES:2 * LANES] = jnp.where(low_half, 1.0, vc).astype(bf16)
            yield

            g_attn = _silu(proj(OFF_GA, ATTN_WIDTH))
            yield

            for n in range(SUB_TILE // BLOCK):
                rows = slice(r0 + n * BLOCK, r0 + (n + 1) * BLOCK)
                keys = slice(r0 + n * BLOCK, r0 + (n + 2) * BLOCK)
                loc = slice(n * BLOCK, (n + 1) * BLOCK)
                first = r0 == 0 and n == 0
                for g in range(N_KV_HEADS):
                    base = g * GROUP * HEAD_DIM
                    q4 = jnp.concatenate(
                        [q_buf[rows, off:off + LANES]
                         for off in (base, base + LANES, ATTN_WIDTH + base, ATTN_WIDTH + base + LANES)], axis=0)
                    vx = vx_buf[g, keys, :]
                    vx = jnp.concatenate(
                        [jnp.where(sink_value_mask, 0.0, vx[0:BF16_ROWS]).astype(bf16), vx[BF16_ROWS:]], axis=0)
                    kz = kz_buf[g, keys, :]
                    kz = jnp.concatenate(
                        [jnp.where(sink_key_mask, 0.0, kz[0:BF16_ROWS]).astype(bf16), kz[BF16_ROWS:]], axis=0)
                    s = lax.dot_general(q4, kz, nt_dims, preferred_element_type=f32)
                    p = []
                    for slab, head in enumerate((0, 2, 1, 3)):
                        sc = s[slab * BLOCK:(slab + 1) * BLOCK]
                        s_prev = sc[:, 0:BLOCK] + prev_bias(g * GROUP + head, first)
                        s_cur = sc[:, BLOCK:2 * BLOCK] + bias_cur
                        m = jnp.max(jnp.maximum(s_prev, s_cur), axis=-1, keepdims=True)
                        p.append(jnp.concatenate(
                            [jnp.exp2(s_prev - m), jnp.exp2(s_cur - m)], axis=1).astype(bf16))
                    pv = jnp.dot(jnp.concatenate(p, axis=0), vx, preferred_element_type=f32)
                    for c in range(2):
                        lo = pv[c * BLOCK:(c + 1) * BLOCK]
                        hi = pv[(2 + c) * BLOCK:(3 + c) * BLOCK]
                        num = jnp.where(low_half_blk, lo[:, 0:LANES], hi[:, LANES:2 * LANES])
                        den = jnp.where(low_half_blk, lo[:, LANES:2 * LANES], hi[:, 0:LANES])
                        cols = slice(base + c * LANES, base + (c + 1) * LANES)
                        mix_buf[rows, cols] = (num / den * g_attn[loc, cols]).astype(bf16)
                    yield

        def conv_side():
            u = proj(OFF_C, CONV_WIDTH) * proj(OFF_H, CONV_WIDTH)
            u0 = CONV_PAD + r0
            u_buf[u0:u0 + SUB_TILE, :] = u
            conv = (taps[0] * u_buf[u0 - 2:u0 - 2 + SUB_TILE, :]
                    + taps[1] * u_buf[u0 - 1:u0 - 1 + SUB_TILE, :]
                    + taps[2] * u)
            yield
            y_conv = proj(OFF_B, CONV_WIDTH) * conv
            yield
            y_conv = y_conv * _silu(proj(OFF_GC, CONV_WIDTH))
            mix_buf[sub, ATTN_WIDTH:ATTN_WIDTH + CONV_WIDTH] = y_conv.astype(bf16)
            yield

        def output():
            h = x + jnp.dot(mix_buf[sub, :], wout_ref[...], preferred_element_type=f32)
            out_ref[sub, :] = _rms_norm(h, final_gain)
            yield

        return {"q": attention_side(), "c": conv_side(), "o": output()}

    assert tile == 2 * SUB_TILE and SUB_TILE == 4 * BLOCK
    streams = {"A": sub_tile(0), "B": sub_tile(SUB_TILE)}
    for token in TRACE_ORDER.split():
        next(streams[token[0]][token[1]])

    u_buf[0:CONV_PAD, :] = u_buf[tile:tile + CONV_PAD, :]
    kz_buf[:, 0:BLOCK, :] = kz_buf[:, tile:tile + BLOCK, :]
    vx_buf[:, 0:BLOCK, :] = vx_buf[:, tile:tile + BLOCK, :]


def _pack_small_params(norm_g, final_g, conv_w):
    taps = jnp.pad(conv_w.astype(jnp.float32), ((0, 0), (0, D_MODEL - CONV_WIDTH)))
    rows = [norm_g.reshape(1, D_MODEL).astype(jnp.float32), final_g.reshape(1, D_MODEL).astype(jnp.float32), taps]
    pad = jnp.zeros((SMALL_ROWS - 2 - CONV_K, D_MODEL), jnp.float32)
    return jnp.concatenate(rows + [pad], axis=0)


@jax.jit
def kernel(x, norm_g, w_in, sinks, conv_w, w_out, final_g):
    batch, seq, d_model = x.shape
    tile = SEQ_TILE
    assert d_model == D_MODEL and seq % tile == 0 and tile % SUB_TILE == 0 and SUB_TILE % BLOCK == 0
    in_width = w_in.shape[1]
    cos_t, sin_lo, sin_hi = _rope_tables(seq)

    grid_spec = pltpu.PrefetchScalarGridSpec(
        num_scalar_prefetch=1,
        grid=(batch, seq // tile),
        in_specs=[
            pl.BlockSpec((None, tile, D_MODEL), lambda b, j, s: (b, j, 0)),
            pl.BlockSpec((SMALL_ROWS, D_MODEL), lambda b, j, s: (0, 0)),
            pl.BlockSpec((D_MODEL, in_width), lambda b, j, s: (0, 0)),
            pl.BlockSpec((D_MODEL, D_MODEL), lambda b, j, s: (0, 0)),
            pl.BlockSpec((tile, LANES), lambda b, j, s: (j, 0)),
            pl.BlockSpec((tile, LANES), lambda b, j, s: (j, 0)),
            pl.BlockSpec((tile, LANES), lambda b, j, s: (j, 0)),
        ],
        out_specs=pl.BlockSpec((None, tile, D_MODEL), lambda b, j, s: (b, j, 0)),
        scratch_shapes=[
            pltpu.VMEM((tile, 2 * ATTN_WIDTH), jnp.bfloat16),
            pltpu.VMEM((N_KV_HEADS, BLOCK + tile, LANES), jnp.bfloat16),
            pltpu.VMEM((N_KV_HEADS, BLOCK + tile, 2 * LANES), jnp.bfloat16),
            pltpu.VMEM((CONV_PAD + tile, CONV_WIDTH), jnp.float32),
            pltpu.VMEM((tile, D_MODEL), jnp.bfloat16),
        ],
    )
    return pl.pallas_call(
        _layer_kernel,
        grid_spec=grid_spec,
        out_shape=jax.ShapeDtypeStruct(x.shape, x.dtype),
        compiler_params=pltpu.CompilerParams(
            dimension_semantics=("arbitrary", "arbitrary"),
            vmem_limit_bytes=VMEM_LIMIT_BYTES),
        name="hybrid_layer",
    )(sinks.astype(jnp.float32), x, _pack_small_params(norm_g, final_g, conv_w),
      w_in.astype(jnp.bfloat16), w_out.astype(jnp.bfloat16),
      cos_t, sin_lo, sin_hi)
```

```python
import functools

import jax
import jax.numpy as jnp
import numpy as np
from jax import lax
from jax.experimental import pallas as pl
from jax.experimental.pallas import tpu as pltpu

D_MODEL = 1024
HEAD_DIM = 64
N_Q_HEADS = 8
N_KV_HEADS = 2
GROUP = N_Q_HEADS // N_KV_HEADS
ATTN_WIDTH = N_Q_HEADS * HEAD_DIM
KV_WIDTH = N_KV_HEADS * HEAD_DIM
BLOCK = 128
ROT_DIM = HEAD_DIM // 4
ROPE_THETA = 500000.0
CONV_WIDTH = D_MODEL - ATTN_WIDTH
CONV_K = 3
EPS = 1e-5
LOG2_E = 1.4426950408889634

LANES = 128
BF16_ROWS = 16
CONV_PAD = 8
SMALL_ROWS = 8
SEQ_TILE = 1024
SUB_TILE = 512
VMEM_LIMIT_BYTES = 56 * 1024 * 1024
TRACE_ORDER = ("Aq Ac Aq Ac Aq Ac "
               "Aq Bq Aq Bq Aq Bq Aq Bc Aq Bc Aq Bc Aq Aq "
               "Bq Bq Ao Bq Bq Bq Bq Bq Bq Bo")

OFF_Q = 0
OFF_KV = OFF_Q + ATTN_WIDTH
OFF_GA = OFF_KV + 2 * KV_WIDTH
OFF_B = OFF_GA + ATTN_WIDTH
OFF_C = OFF_B + CONV_WIDTH
OFF_H = OFF_C + CONV_WIDTH
OFF_GC = OFF_H + CONV_WIDTH


@functools.lru_cache(maxsize=None)
def _rope_tables(seq):
    half = ROT_DIM // 2
    inv_freq = ROPE_THETA ** (-np.arange(0, ROT_DIM, 2, dtype=np.float64) / ROT_DIM)
    ang = np.arange(seq, dtype=np.float64)[:, None] * inv_freq[None, :]
    cos, sin = np.cos(ang), np.sin(ang)
    ones = np.ones((seq, HEAD_DIM - ROT_DIM))
    zeros_h = np.zeros((seq, half))
    zeros_t = np.zeros((seq, HEAD_DIM - ROT_DIM))
    cos_t = np.concatenate([cos, cos, ones], axis=1)
    sin_lo = np.concatenate([-sin, zeros_h, zeros_t], axis=1)
    sin_hi = np.concatenate([zeros_h, sin, zeros_t], axis=1)
    rep = LANES // HEAD_DIM
    return tuple(np.tile(t, (1, rep)).astype(np.float32) for t in (cos_t, sin_lo, sin_hi))


def _rms_norm(x, g):
    ms = jnp.mean(x * x, axis=-1, keepdims=True)
    return x * lax.rsqrt(ms + EPS) * g


def _silu(x):
    h = 0.5 * x
    return h + h * jnp.tanh(h)


def _layer_kernel(sinks_ref, x_ref, par_ref, win_ref, wout_ref,
                  cos_ref, sinlo_ref, sinhi_ref, out_ref,
                  q_buf, kz_buf, vx_buf, u_buf, mix_buf):
    tile = x_ref.shape[0]
    j = pl.program_id(1)
    norm_gain, final_gain = par_ref[0:1, :], par_ref[1:2, :]
    taps = [par_ref[2 + k:3 + k, 0:CONV_WIDTH] for k in range(CONV_K)]
    f32, bf16 = jnp.float32, jnp.bfloat16

    @pl.when(j == 0)
    def _():
        kz_buf[:, 0:BLOCK, :] = jnp.zeros((N_KV_HEADS, BLOCK, LANES), bf16)
        vx_buf[:, 0:BLOCK, :] = jnp.ones((N_KV_HEADS, BLOCK, 2 * LANES), bf16)
        u_buf[0:CONV_PAD, :] = jnp.zeros((CONV_PAD, CONV_WIDTH), f32)

    low_half = lax.broadcasted_iota(jnp.int32, (SUB_TILE, LANES), 1) < HEAD_DIM
    qi = lax.broadcasted_iota(jnp.int32, (BLOCK, BLOCK), 0)
    kj = lax.broadcasted_iota(jnp.int32, (BLOCK, BLOCK), 1)
    neg_inf = jnp.float32(-jnp.inf)
    bias_cur = jnp.where(kj <= qi, 0.0, neg_inf)
    bias_prev_mid = jnp.where(kj > qi, 0.0, neg_inf)
    bias_prev_first = jnp.where((kj > qi) & (j > 0), 0.0, neg_inf)
    sink_slot = kj == 0
    bias_prev_sink = {}

    def prev_bias(head, first):
        if (head, first) not in bias_prev_sink:
            bias_prev_sink[head, first] = jnp.where(
                sink_slot, sinks_ref[head] * LOG2_E, bias_prev_first if first else bias_prev_mid)
        return bias_prev_sink[head, first]

    low_half_blk = kj < HEAD_DIM
    top_r = lax.broadcasted_iota(jnp.int32, (BF16_ROWS, 2 * LANES), 0)
    top_c = lax.broadcasted_iota(jnp.int32, (BF16_ROWS, 2 * LANES), 1)
    sink_value_mask = (top_r == 0) & ((top_c < HEAD_DIM) | (top_c >= 2 * LANES - HEAD_DIM))
    sink_key_mask = lax.broadcasted_iota(jnp.int32, (BF16_ROWS, LANES), 0) == 0
    nt_dims = (((1,), (1,)), ((), ()))
    scale = HEAD_DIM ** -0.5 * LOG2_E

    def sub_tile(r0):
        sub = slice(r0, r0 + SUB_TILE)
        x = x_ref[sub, :]
        xn = _rms_norm(x, norm_gain).astype(bf16)

        def proj(off, width):
            return jnp.dot(xn, win_ref[:, off:off + width], preferred_element_type=f32)

        def attention_side():
            cos_t, sin_lo, sin_hi = cos_ref[sub, :], sinlo_ref[sub, :], sinhi_ref[sub, :]

            def rope(t):
                return (t * cos_t + pltpu.roll(t, LANES - ROT_DIM // 2, 1) * sin_lo
                        + pltpu.roll(t, ROT_DIM // 2, 1) * sin_hi)

            q = proj(OFF_Q, ATTN_WIDTH)
            for c in range(ATTN_WIDTH // LANES):
                qc = rope(q[:, c * LANES:(c + 1) * LANES]) * scale
                q_buf[sub, c * LANES:(c + 1) * LANES] = qc.astype(bf16)
                q_buf[sub, ATTN_WIDTH + c * LANES:ATTN_WIDTH + (c + 1) * LANES] = (
                    pltpu.roll(qc, HEAD_DIM, 1).astype(bf16))
            yield

            kv = proj(OFF_KV, 2 * KV_WIDTH)
            new_rows = slice(BLOCK + r0, BLOCK + r0 + SUB_TILE)
            kr = rope(kv[:, 0:KV_WIDTH])
            kr_sw = pltpu.roll(kr, HEAD_DIM, 1)
            kz_buf[0, new_rows, :] = jnp.where(low_half, kr, 0.0).astype(bf16)
            kz_buf[1, new_rows, :] = jnp.where(low_half, kr_sw, 0.0).astype(bf16)
            vc = kv[:, KV_WIDTH:2 * KV_WIDTH]
            vc_sw = pltpu.roll(vc, HEAD_DIM, 1)
            vx_buf[0, new_rows, 0:LANES] = jnp.where(low_half, vc, 1.0).astype(bf16)
            vx_buf[0, new_rows, LANES:2 * LANES] = jnp.where(low_half, 1.0, vc_sw).astype(bf16)
            vx_buf[1, new_rows, 0:LANES] = jnp.where(low_half, vc_sw, 1.0).astype(bf16)
            vx_buf[1, new_rows, LANES:2 * LANES] = jnp.where(low_half, 1.0, vc).astype(bf16)
            yield

            g_attn = _silu(proj(OFF_GA, ATTN_WIDTH))
            yield

            for n in range(SUB_TILE // BLOCK):
                rows = slice(r0 + n * BLOCK, r0 + (n + 1) * BLOCK)
                keys = slice(r0 + n * BLOCK, r0 + (n + 2) * BLOCK)
                loc = slice(n * BLOCK, (n + 1) * BLOCK)
                first = r0 == 0 and n == 0
                for g in range(N_KV_HEADS):
                    base = g * GROUP * HEAD_DIM
                    q4 = jnp.concatenate(
                        [q_buf[rows, off:off + LANES]
                         for off in (base, base + LANES, ATTN_WIDTH + base, ATTN_WIDTH + base + LANES)], axis=0)
                    vx = vx_buf[g, keys, :]
                    vx = jnp.concatenate(
                        [jnp.where(sink_value_mask, 0.0, vx[0:BF16_ROWS]).astype(bf16), vx[BF16_ROWS:]], axis=0)
                    kz = kz_buf[g, keys, :]
                    kz = jnp.concatenate(
                        [jnp.where(sink_key_mask, 0.0, kz[0:BF16_ROWS]).astype(bf16), kz[BF16_ROWS:]], axis=0)
                    s = lax.dot_general(q4, kz, nt_dims, preferred_element_type=f32)
                    p = []
                    for slab, head in enumerate((0, 2, 1, 3)):
                        sc = s[slab * BLOCK:(slab + 1) * BLOCK]
                        s_prev = sc[:, 0:BLOCK] + prev_bias(g * GROUP + head, first)
                        s_cur = sc[:, BLOCK:2 * BLOCK] + bias_cur
                        m = jnp.max(jnp.maximum(s_prev, s_cur), axis=-1, keepdims=True)
                        p.append(jnp.concatenate(
                            [jnp.exp2(s_prev - m), jnp.exp2(s_cur - m)], axis=1).astype(bf16))
                    pv = jnp.dot(jnp.concatenate(p, axis=0), vx, preferred_element_type=f32)
                    for c in range(2):
                        lo = pv[c * BLOCK:(c + 1) * BLOCK]
                        hi = pv[(2 + c) * BLOCK:(3 + c) * BLOCK]
                        num = jnp.where(low_half_blk, lo[:, 0:LANES], hi[:, LANES:2 * LANES])
                        den = jnp.where(low_half_blk, lo[:, LANES:2 * LANES], hi[:, 0:LANES])
                        cols = slice(base + c * LANES, base + (c + 1) * LANES)
                        mix_buf[rows, cols] = (num / den * g_attn[loc, cols]).astype(bf16)
                    yield

        def conv_side():
            u = proj(OFF_C, CONV_WIDTH) * proj(OFF_H, CONV_WIDTH)
            u0 = CONV_PAD + r0
            u_buf[u0:u0 + SUB_TILE, :] = u
            conv = (taps[0] * u_buf[u0 - 2:u0 - 2 + SUB_TILE, :]
                    + taps[1] * u_buf[u0 - 1:u0 - 1 + SUB_TILE, :]
                    + taps[2] * u)
            yield
            y_conv = proj(OFF_B, CONV_WIDTH) * conv
            yield
            y_conv = y_conv * _silu(proj(OFF_GC, CONV_WIDTH))
            mix_buf[sub, ATTN_WIDTH:ATTN_WIDTH + CONV_WIDTH] = y_conv.astype(bf16)
            yield

        def output():
            h = x + jnp.dot(mix_buf[sub, :], wout_ref[...], preferred_element_type=f32)
            out_ref[sub, :] = _rms_norm(h, final_gain)
            yield

        return {"q": attention_side(), "c": conv_side(), "o": output()}

    assert tile == 2 * SUB_TILE and SUB_TILE == 4 * BLOCK
    streams = {"A": sub_tile(0), "B": sub_tile(SUB_TILE)}
    for token in TRACE_ORDER.split():
        next(streams[token[0]][token[1]])

    u_buf[0:CONV_PAD, :] = u_buf[tile:tile + CONV_PAD, :]
    kz_buf[:, 0:BLOCK, :] = kz_buf[:, tile:tile + BLOCK, :]
    vx_buf[:, 0:BLOCK, :] = vx_buf[:, tile:tile + BLOCK, :]


def _pack_small_params(norm_g, final_g, conv_w):
    taps = jnp.pad(conv_w.astype(jnp.float32), ((0, 0), (0, D_MODEL - CONV_WIDTH)))
    rows = [norm_g.reshape(1, D_MODEL).astype(jnp.float32), final_g.reshape(1, D_MODEL).astype(jnp.float32), taps]
    pad = jnp.zeros((SMALL_ROWS - 2 - CONV_K, D_MODEL), jnp.float32)
    return jnp.concatenate(rows + [pad], axis=0)


@jax.jit
def kernel(x, norm_g, w_in, sinks, conv_w, w_out, final_g):
    batch, seq, d_model = x.shape
    tile = SEQ_TILE
    assert d_model == D_MODEL and seq % tile == 0 and tile % SUB_TILE == 0 and SUB_TILE % BLOCK == 0
    in_width = w_in.shape[1]
    cos_t, sin_lo, sin_hi = _rope_tables(seq)

    grid_spec = pltpu.PrefetchScalarGridSpec(
        num_scalar_prefetch=1,
        grid=(batch, seq // tile),
        in_specs=[
            pl.BlockSpec((None, tile, D_MODEL), lambda b, j, s: (b, j, 0)),
            pl.BlockSpec((SMALL_ROWS, D_MODEL), lambda b, j, s: (0, 0)),
            pl.BlockSpec((D_MODEL, in_width), lambda b, j, s: (0, 0)),
            pl.BlockSpec((D_MODEL, D_MODEL), lambda b, j, s: (0, 0)),
            pl.BlockSpec((tile, LANES), lambda b, j, s: (j, 0)),
            pl.BlockSpec((tile, LANES), lambda b, j, s: (j, 0)),
            pl.BlockSpec((tile, LANES), lambda b, j, s: (j, 0)),
        ],
        out_specs=pl.BlockSpec((None, tile, D_MODEL), lambda b, j, s: (b, j, 0)),
        scratch_shapes=[
            pltpu.VMEM((tile, 2 * ATTN_WIDTH), jnp.bfloat16),
            pltpu.VMEM((N_KV_HEADS, BLOCK + tile, LANES), jnp.bfloat16),
            pltpu.VMEM((N_KV_HEADS, BLOCK + tile, 2 * LANES), jnp.bfloat16),
            pltpu.VMEM((CONV_PAD + tile, CONV_WIDTH), jnp.float32),
            pltpu.VMEM((tile, D_MODEL), jnp.bfloat16),
        ],
    )
    return pl.pallas_call(
        _layer_kernel,
        grid_spec=grid_spec,
        out_shape=jax.ShapeDtypeStruct(x.shape, x.dtype),
        compiler_params=pltpu.CompilerParams(
            dimension_semantics=("arbitrary", "arbitrary"),
            vmem_limit_bytes=VMEM_LIMIT_BYTES),
        name="hybrid_layer",
    )(sinks.astype(jnp.float32), x, _pack_small_params(norm_g, final_g, conv_w),
      w_in.astype(jnp.bfloat16), w_out.astype(jnp.bfloat16),
      cos_t, sin_lo, sin_hi)
```

```python
import functools

import jax
import jax.numpy as jnp
import numpy as np
from jax import lax
from jax.experimental import pallas as pl
from jax.experimental.pallas import tpu as pltpu

D_MODEL = 1024
HEAD_DIM = 64
N_Q_HEADS = 8
N_KV_HEADS = 2
GROUP = N_Q_HEADS // N_KV_HEADS
ATTN_WIDTH = N_Q_HEADS * HEAD_DIM
KV_WIDTH = N_KV_HEADS * HEAD_DIM
BLOCK = 128
ROT_DIM = HEAD_DIM // 4
ROPE_THETA = 500000.0
CONV_WIDTH = D_MODEL - ATTN_WIDTH
CONV_K = 3
EPS = 1e-5
LOG2_E = 1.4426950408889634

LANES = 128
N_TILE = 256
BF16_ROWS = 16
CONV_PAD = 8
SMALL_ROWS = 8
SEQ_TILE = 1024
SUB_TILE = 512
VMEM_LIMIT_BYTES = 56 * 1024 * 1024
TRACE_ORDER = ("Aq Aq Aq Ac Ac Ac "
               "Aq Bq Aq Bq Aq Bq Aq Bc Aq Bc Aq Bc Aq Aq "
               "Bq Bq Ao Bq Bq Bq Bq Bq Bq Bo")

OFF_Q = 0
OFF_KV = OFF_Q + ATTN_WIDTH
OFF_GA = OFF_KV + 2 * KV_WIDTH
OFF_B = OFF_GA + ATTN_WIDTH
OFF_C = OFF_B + CONV_WIDTH
OFF_H = OFF_C + CONV_WIDTH
OFF_GC = OFF_H + CONV_WIDTH


@functools.lru_cache(maxsize=None)
def _rope_tables(seq):
    half = ROT_DIM // 2
    inv_freq = ROPE_THETA ** (-np.arange(0, ROT_DIM, 2, dtype=np.float64) / ROT_DIM)
    ang = np.arange(seq, dtype=np.float64)[:, None] * inv_freq[None, :]
    cos, sin = np.cos(ang), np.sin(ang)
    ones = np.ones((seq, HEAD_DIM - ROT_DIM))
    zeros_h = np.zeros((seq, half))
    zeros_t = np.zeros((seq, HEAD_DIM - ROT_DIM))
    cos_t = np.concatenate([cos, cos, ones], axis=1)
    sin_lo = np.concatenate([-sin, zeros_h, zeros_t], axis=1)
    sin_hi = np.concatenate([zeros_h, sin, zeros_t], axis=1)
    rep = LANES // HEAD_DIM
    return tuple(np.tile(t, (1, rep)).astype(np.float32) for t in (cos_t, sin_lo, sin_hi))


def _rms_norm(x, g):
    ms = jnp.mean(x * x, axis=-1, keepdims=True)
    return x * lax.rsqrt(ms + EPS) * g


def _silu(x):
    h = 0.5 * x
    return h + h * jnp.tanh(h)


def _layer_kernel(sinks_ref, x_ref, par_ref, win_ref, wout_ref,
                  cos_ref, sinlo_ref, sinhi_ref, out_ref,
                  q_buf, kz_buf, vx_buf, u_buf, mix_buf):
    tile = x_ref.shape[0]
    j = pl.program_id(1)
    norm_gain, final_gain = par_ref[0:1, :], par_ref[1:2, :]
    taps = [par_ref[2 + k:3 + k, 0:CONV_WIDTH] for k in range(CONV_K)]
    f32, bf16 = jnp.float32, jnp.bfloat16

    @pl.when(j == 0)
    def _():
        kz_buf[:, 0:BLOCK, :] = jnp.zeros((N_KV_HEADS, BLOCK, LANES), bf16)
        vx_buf[:, 0:BLOCK, :] = jnp.ones((N_KV_HEADS, BLOCK, 2 * LANES), bf16)
        u_buf[0:CONV_PAD, :] = jnp.zeros((CONV_PAD, CONV_WIDTH), f32)

    low_half = lax.broadcasted_iota(jnp.int32, (SUB_TILE, LANES), 1) < HEAD_DIM
    qi = lax.broadcasted_iota(jnp.int32, (BLOCK, BLOCK), 0)
    kj = lax.broadcasted_iota(jnp.int32, (BLOCK, BLOCK), 1)
    neg_inf = jnp.float32(-jnp.inf)
    bias_cur = jnp.where(kj <= qi, 0.0, neg_inf)
    bias_prev_mid = jnp.where(kj > qi, 0.0, neg_inf)
    bias_prev_first = jnp.where((kj > qi) & (j > 0), 0.0, neg_inf)
    sink_slot = kj == 0
    bias_prev_sink = {}

    def prev_bias(head, first):
        if (head, first) not in bias_prev_sink:
            bias_prev_sink[head, first] = jnp.where(
                sink_slot, sinks_ref[head] * LOG2_E, bias_prev_first if first else bias_prev_mid)
        return bias_prev_sink[head, first]

    low_half_blk = kj < HEAD_DIM
    top_r = lax.broadcasted_iota(jnp.int32, (BF16_ROWS, 2 * LANES), 0)
    top_c = lax.broadcasted_iota(jnp.int32, (BF16_ROWS, 2 * LANES), 1)
    sink_value_mask = (top_r == 0) & ((top_c < HEAD_DIM) | (top_c >= 2 * LANES - HEAD_DIM))
    sink_key_mask = lax.broadcasted_iota(jnp.int32, (BF16_ROWS, LANES), 0) == 0
    nt_dims = (((1,), (1,)), ((), ()))
    scale = HEAD_DIM ** -0.5 * LOG2_E

    def sub_tile(r0):
        sub = slice(r0, r0 + SUB_TILE)
        x = x_ref[sub, :]
        xn = _rms_norm(x, norm_gain).astype(bf16)

        def proj(off, width):
            return jnp.dot(xn, win_ref[:, off:off + width], preferred_element_type=f32)

        def proj_tiles(fn, *offs, width=CONV_WIDTH):
            return jnp.concatenate(
                [fn(*[proj(off + t, N_TILE) for off in offs], t) for t in range(0, width, N_TILE)], axis=1)

        def attention_side():
            cos_t, sin_lo, sin_hi = cos_ref[sub, :], sinlo_ref[sub, :], sinhi_ref[sub, :]

            def rope(t):
                return (t * cos_t + pltpu.roll(t, LANES - ROT_DIM // 2, 1) * sin_lo
                        + pltpu.roll(t, ROT_DIM // 2, 1) * sin_hi)

            q = proj(OFF_Q, ATTN_WIDTH)
            for c in range(ATTN_WIDTH // LANES):
                qc = rope(q[:, c * LANES:(c + 1) * LANES]) * scale
                q_buf[sub, c * LANES:(c + 1) * LANES] = qc.astype(bf16)
                q_buf[sub, ATTN_WIDTH + c * LANES:ATTN_WIDTH + (c + 1) * LANES] = (
                    pltpu.roll(qc, HEAD_DIM, 1).astype(bf16))
            yield

            kv = proj(OFF_KV, 2 * KV_WIDTH)
            new_rows = slice(BLOCK + r0, BLOCK + r0 + SUB_TILE)
            kr = rope(kv[:, 0:KV_WIDTH])
            kr_sw = pltpu.roll(kr, HEAD_DIM, 1)
            kz_buf[0, new_rows, :] = jnp.where(low_half, kr, 0.0).astype(bf16)
            kz_buf[1, new_rows, :] = jnp.where(low_half, kr_sw, 0.0).astype(bf16)
            vc = kv[:, KV_WIDTH:2 * KV_WIDTH]
            vc_sw = pltpu.roll(vc, HEAD_DIM, 1)
            vx_buf[0, new_rows, 0:LANES] = jnp.where(low_half, vc, 1.0).astype(bf16)
            vx_buf[0, new_rows, LANES:2 * LANES] = jnp.where(low_half, 1.0, vc_sw).astype(bf16)
            vx_buf[1, new_rows, 0:LANES] = jnp.where(low_half, vc_sw, 1.0).astype(bf16)
            vx_buf[1, new_rows, LANES:2 * LANES] = jnp.where(low_half, 1.0, vc).astype(bf16)
            yield

            g_attn = proj_tiles(lambda ga, t: _silu(ga), OFF_GA, width=ATTN_WIDTH)
            yield

            for n in range(SUB_TILE // BLOCK):
                rows = slice(r0 + n * BLOCK, r0 + (n + 1) * BLOCK)
                keys = slice(r0 + n * BLOCK, r0 + (n + 2) * BLOCK)
                loc = slice(n * BLOCK, (n + 1) * BLOCK)
                first = r0 == 0 and n == 0
                for g in range(N_KV_HEADS):
                    base = g * GROUP * HEAD_DIM
                    q4 = jnp.concatenate(
                        [q_buf[rows, off:off + LANES]
                         for off in (base, base + LANES, ATTN_WIDTH + base, ATTN_WIDTH + base + LANES)], axis=0)
                    vx = vx_buf[g, keys, :]
                    vx = jnp.concatenate(
                        [jnp.where(sink_value_mask, 0.0, vx[0:BF16_ROWS]).astype(bf16), vx[BF16_ROWS:]], axis=0)
                    kz = kz_buf[g, keys, :]
                    kz = jnp.concatenate(
                        [jnp.where(sink_key_mask, 0.0, kz[0:BF16_ROWS]).astype(bf16), kz[BF16_ROWS:]], axis=0)
                    s = lax.dot_general(q4, kz, nt_dims, preferred_element_type=f32)
                    p = []
                    for slab, head in enumerate((0, 2, 1, 3)):
                        sc = s[slab * BLOCK:(slab + 1) * BLOCK]
                        s_prev = sc[:, 0:BLOCK] + prev_bias(g * GROUP + head, first)
                        s_cur = sc[:, BLOCK:2 * BLOCK] + bias_cur
                        m = jnp.max(jnp.maximum(s_prev, s_cur), axis=-1, keepdims=True)
                        p.append(jnp.concatenate(
                            [jnp.exp2(s_prev - m), jnp.exp2(s_cur - m)], axis=1).astype(bf16))
                    pv = jnp.dot(jnp.concatenate(p, axis=0), vx, preferred_element_type=f32)
                    for c in range(2):
                        lo = pv[c * BLOCK:(c + 1) * BLOCK]
                        hi = pv[(2 + c) * BLOCK:(3 + c) * BLOCK]
                        num = jnp.where(low_half_blk, lo[:, 0:LANES], hi[:, LANES:2 * LANES])
                        den = jnp.where(low_half_blk, lo[:, LANES:2 * LANES], hi[:, 0:LANES])
                        cols = slice(base + c * LANES, base + (c + 1) * LANES)
                        mix_buf[rows, cols] = (num / den * g_attn[loc, cols]).astype(bf16)
                    yield

        def conv_side():
            u = proj_tiles(lambda c, h, t: c * h, OFF_C, OFF_H)
            u0 = CONV_PAD + r0
            u_buf[u0:u0 + SUB_TILE, :] = u
            conv = (taps[0] * u_buf[u0 - 2:u0 - 2 + SUB_TILE, :]
                    + taps[1] * u_buf[u0 - 1:u0 - 1 + SUB_TILE, :]
                    + taps[2] * u)
            yield
            y_conv = proj_tiles(lambda b, t: b * conv[:, t:t + N_TILE], OFF_B)
            yield
            y_conv = proj_tiles(lambda gc, t: y_conv[:, t:t + N_TILE] * _silu(gc), OFF_GC)
            mix_buf[sub, ATTN_WIDTH:ATTN_WIDTH + CONV_WIDTH] = y_conv.astype(bf16)
            yield

        def output():
            h = x + jnp.dot(mix_buf[sub, :], wout_ref[...], preferred_element_type=f32)
            out_ref[sub, :] = _rms_norm(h, final_gain)
            yield

        return {"q": attention_side(), "c": conv_side(), "o": output()}

    assert tile == 2 * SUB_TILE and SUB_TILE == 4 * BLOCK
    streams = {"A": sub_tile(0), "B": sub_tile(SUB_TILE)}
    for token in TRACE_ORDER.split():
        next(streams[token[0]][token[1]])

    u_buf[0:CONV_PAD, :] = u_buf[tile:tile + CONV_PAD, :]
    kz_buf[:, 0:BLOCK, :] = kz_buf[:, tile:tile + BLOCK, :]
    vx_buf[:, 0:BLOCK, :] = vx_buf[:, tile:tile + BLOCK, :]


def _pack_small_params(norm_g, final_g, conv_w):
    taps = jnp.pad(conv_w.astype(jnp.float32), ((0, 0), (0, D_MODEL - CONV_WIDTH)))
    rows = [norm_g.reshape(1, D_MODEL).astype(jnp.float32), final_g.reshape(1, D_MODEL).astype(jnp.float32), taps]
    pad = jnp.zeros((SMALL_ROWS - 2 - CONV_K, D_MODEL), jnp.float32)
    return jnp.concatenate(rows + [pad], axis=0)


@jax.jit
def kernel(x, norm_g, w_in, sinks, conv_w, w_out, final_g):
    batch, seq, d_model = x.shape
    tile = SEQ_TILE
    assert d_model == D_MODEL and seq % tile == 0 and tile % SUB_TILE == 0 and SUB_TILE % BLOCK == 0
    in_width = w_in.shape[1]
    cos_t, sin_lo, sin_hi = _rope_tables(seq)

    grid_spec = pltpu.PrefetchScalarGridSpec(
        num_scalar_prefetch=1,
        grid=(batch, seq // tile),
        in_specs=[
            pl.BlockSpec((None, tile, D_MODEL), lambda b, j, s: (b, j, 0)),
            pl.BlockSpec((SMALL_ROWS, D_MODEL), lambda b, j, s: (0, 0)),
            pl.BlockSpec((D_MODEL, in_width), lambda b, j, s: (0, 0)),
            pl.BlockSpec((D_MODEL, D_MODEL), lambda b, j, s: (0, 0)),
            pl.BlockSpec((tile, LANES), lambda b, j, s: (j, 0)),
            pl.BlockSpec((tile, LANES), lambda b, j, s: (j, 0)),
            pl.BlockSpec((tile, LANES), lambda b, j, s: (j, 0)),
        ],
        out_specs=pl.BlockSpec((None, tile, D_MODEL), lambda b, j, s: (b, j, 0)),
        scratch_shapes=[
            pltpu.VMEM((tile, 2 * ATTN_WIDTH), jnp.bfloat16),
            pltpu.VMEM((N_KV_HEADS, BLOCK + tile, LANES), jnp.bfloat16),
            pltpu.VMEM((N_KV_HEADS, BLOCK + tile, 2 * LANES), jnp.bfloat16),
            pltpu.VMEM((CONV_PAD + tile, CONV_WIDTH), jnp.float32),
            pltpu.VMEM((tile, D_MODEL), jnp.bfloat16),
        ],
    )
    return pl.pallas_call(
        _layer_kernel,
        grid_spec=grid_spec,
        out_shape=jax.ShapeDtypeStruct(x.shape, x.dtype),
        compiler_params=pltpu.CompilerParams(
            dimension_semantics=("arbitrary", "arbitrary"),
            vmem_limit_bytes=VMEM_LIMIT_BYTES),
        name="hybrid_layer",
    )(sinks.astype(jnp.float32), x, _pack_small_params(norm_g, final_g, conv_w),
      w_in.astype(jnp.bfloat16), w_out.astype(jnp.bfloat16),
      cos_t, sin_lo, sin_hi)
```

```python
import functools

import jax
import jax.numpy as jnp
import numpy as np
from jax import lax
from jax.experimental import pallas as pl
from jax.experimental.pallas import tpu as pltpu

D_MODEL = 1024
HEAD_DIM = 64
N_Q_HEADS = 8
N_KV_HEADS = 2
GROUP = N_Q_HEADS // N_KV_HEADS
ATTN_WIDTH = N_Q_HEADS * HEAD_DIM
KV_WIDTH = N_KV_HEADS * HEAD_DIM
BLOCK = 128
ROT_DIM = HEAD_DIM // 4
ROPE_THETA = 500000.0
CONV_WIDTH = D_MODEL - ATTN_WIDTH
CONV_K = 3
EPS = 1e-5
LOG2_E = 1.4426950408889634

LANES = 128
N_TILE = 256
BF16_ROWS = 16
CONV_PAD = 8
SMALL_ROWS = 8
SEQ_TILE = 1024
SUB_TILE = 512
VMEM_LIMIT_BYTES = 56 * 1024 * 1024
TRACE_ORDER = ("Aq Aq Aq Ac Ac Ac "
               "Aq Bq Aq Bq Aq Bq Aq Bc Aq Bc Aq Bc Aq Aq "
               "Bq Bq Ao Bq Bq Bq Bq Bq Bq Bo")

OFF_Q = 0
OFF_KV = OFF_Q + ATTN_WIDTH
OFF_GA = OFF_KV + 2 * KV_WIDTH
OFF_B = OFF_GA + ATTN_WIDTH
OFF_C = OFF_B + CONV_WIDTH
OFF_H = OFF_C + CONV_WIDTH
OFF_GC = OFF_H + CONV_WIDTH


@functools.lru_cache(maxsize=None)
def _rope_tables(seq):
    half = ROT_DIM // 2
    inv_freq = ROPE_THETA ** (-np.arange(0, ROT_DIM, 2, dtype=np.float64) / ROT_DIM)
    ang = np.arange(seq, dtype=np.float64)[:, None] * inv_freq[None, :]
    cos, sin = np.cos(ang), np.sin(ang)
    ones = np.ones((seq, HEAD_DIM - ROT_DIM))
    zeros_h = np.zeros((seq, half))
    zeros_t = np.zeros((seq, HEAD_DIM - ROT_DIM))
    cos_t = np.concatenate([cos, cos, ones], axis=1)
    sin_lo = np.concatenate([-sin, zeros_h, zeros_t], axis=1)
    sin_hi = np.concatenate([zeros_h, sin, zeros_t], axis=1)
    rep = LANES // HEAD_DIM
    return tuple(np.tile(t, (1, rep)).astype(np.float32) for t in (cos_t, sin_lo, sin_hi))


def _rms_norm(x, g):
    ms = jnp.mean(x * x, axis=-1, keepdims=True)
    return x * lax.rsqrt(ms + EPS) * g


def _silu(x):
    h = 0.5 * x
    return h + h * jnp.tanh(h)


def _layer_kernel(sinks_ref, x_ref, par_ref, win_ref, wout_ref,
                  cos_ref, sinlo_ref, sinhi_ref, out_ref,
                  q_buf, kz_buf, vx_buf, u_buf, mix_buf):
    tile = x_ref.shape[0]
    j = pl.program_id(1)
    norm_gain, final_gain = par_ref[0:1, :], par_ref[1:2, :]
    taps = [par_ref[2 + k:3 + k, 0:CONV_WIDTH] for k in range(CONV_K)]
    f32, bf16 = jnp.float32, jnp.bfloat16

    @pl.when(j == 0)
    def _():
        kz_buf[:, 0:BLOCK, :] = jnp.zeros((N_KV_HEADS, BLOCK, LANES), bf16)
        vx_buf[:, 0:BLOCK, :] = jnp.ones((N_KV_HEADS, BLOCK, 2 * LANES), bf16)
        u_buf[0:CONV_PAD, :] = jnp.zeros((CONV_PAD, CONV_WIDTH), f32)

    low_half = lax.broadcasted_iota(jnp.int32, (SUB_TILE, LANES), 1) < HEAD_DIM
    qi = lax.broadcasted_iota(jnp.int32, (BLOCK, BLOCK), 0)
    kj = lax.broadcasted_iota(jnp.int32, (BLOCK, BLOCK), 1)
    neg_inf = jnp.float32(-jnp.inf)
    bias_cur = jnp.where(kj <= qi, 0.0, neg_inf)
    bias_prev_mid = jnp.where(kj > qi, 0.0, neg_inf)
    bias_prev_first = jnp.where((kj > qi) & (j > 0), 0.0, neg_inf)
    sink_slot = kj == 0
    bias_prev_sink = {}

    def prev_bias(head, first):
        if (head, first) not in bias_prev_sink:
            bias_prev_sink[head, first] = jnp.where(
                sink_slot, sinks_ref[head] * LOG2_E, bias_prev_first if first else bias_prev_mid)
        return bias_prev_sink[head, first]

    low_half_blk = kj < HEAD_DIM
    top_r = lax.broadcasted_iota(jnp.int32, (BF16_ROWS, 2 * LANES), 0)
    top_c = lax.broadcasted_iota(jnp.int32, (BF16_ROWS, 2 * LANES), 1)
    sink_value_mask = (top_r == 0) & ((top_c < HEAD_DIM) | (top_c >= 2 * LANES - HEAD_DIM))
    sink_key_mask = lax.broadcasted_iota(jnp.int32, (BF16_ROWS, LANES), 0) == 0
    nt_dims = (((1,), (1,)), ((), ()))
    scale = HEAD_DIM ** -0.5 * LOG2_E

    def sub_tile(r0):
        sub = slice(r0, r0 + SUB_TILE)
        x = x_ref[sub, :]
        xn = _rms_norm(x, norm_gain).astype(bf16)

        def proj(off, width):
            return jnp.dot(xn, win_ref[:, off:off + width], preferred_element_type=f32)

        def proj_tiles(fn, *offs, width=CONV_WIDTH):
            return jnp.concatenate(
                [fn(*[proj(off + t, N_TILE) for off in offs], t) for t in range(0, width, N_TILE)], axis=1)

        def attention_side():
            cos_t, sin_lo, sin_hi = cos_ref[sub, :], sinlo_ref[sub, :], sinhi_ref[sub, :]

            def rope(t):
                return (t * cos_t + pltpu.roll(t, LANES - ROT_DIM // 2, 1) * sin_lo
                        + pltpu.roll(t, ROT_DIM // 2, 1) * sin_hi)

            for c in range(ATTN_WIDTH // LANES):
                if c % (N_TILE // LANES) == 0:
                    q = proj(OFF_Q + c * LANES, N_TILE)
                lane0 = (c % (N_TILE // LANES)) * LANES
                qc = rope(q[:, lane0:lane0 + LANES]) * scale
                q_buf[sub, c * LANES:(c + 1) * LANES] = qc.astype(bf16)
                q_buf[sub, ATTN_WIDTH + c * LANES:ATTN_WIDTH + (c + 1) * LANES] = (
                    pltpu.roll(qc, HEAD_DIM, 1).astype(bf16))
            yield

            kv = proj(OFF_KV, 2 * KV_WIDTH)
            new_rows = slice(BLOCK + r0, BLOCK + r0 + SUB_TILE)
            kr = rope(kv[:, 0:KV_WIDTH])
            kr_sw = pltpu.roll(kr, HEAD_DIM, 1)
            kz_buf[0, new_rows, :] = jnp.where(low_half, kr, 0.0).astype(bf16)
            kz_buf[1, new_rows, :] = jnp.where(low_half, kr_sw, 0.0).astype(bf16)
            vc = kv[:, KV_WIDTH:2 * KV_WIDTH]
            vc_sw = pltpu.roll(vc, HEAD_DIM, 1)
            vx_buf[0, new_rows, 0:LANES] = jnp.where(low_half, vc, 1.0).astype(bf16)
            vx_buf[0, new_rows, LANES:2 * LANES] = jnp.where(low_half, 1.0, vc_sw).astype(bf16)
            vx_buf[1, new_rows, 0:LANES] = jnp.where(low_half, vc_sw, 1.0).astype(bf16)
            vx_buf[1, new_rows, LANES:2 * LANES] = jnp.where(low_half, 1.0, vc).astype(bf16)
            yield

            g_attn = proj_tiles(lambda ga, t: _silu(ga), OFF_GA, width=ATTN_WIDTH)
            yield

            for n in range(SUB_TILE // BLOCK):
                rows = slice(r0 + n * BLOCK, r0 + (n + 1) * BLOCK)
                keys = slice(r0 + n * BLOCK, r0 + (n + 2) * BLOCK)
                loc = slice(n * BLOCK, (n + 1) * BLOCK)
                first = r0 == 0 and n == 0
                for g in range(N_KV_HEADS):
                    base = g * GROUP * HEAD_DIM
                    q4 = jnp.concatenate(
                        [q_buf[rows, off:off + LANES]
                         for off in (base, base + LANES, ATTN_WIDTH + base, ATTN_WIDTH + base + LANES)], axis=0)
                    vx = vx_buf[g, keys, :]
                    vx = jnp.concatenate(
                        [jnp.where(sink_value_mask, 0.0, vx[0:BF16_ROWS]).astype(bf16), vx[BF16_ROWS:]], axis=0)
                    kz = kz_buf[g, keys, :]
                    kz = jnp.concatenate(
                        [jnp.where(sink_key_mask, 0.0, kz[0:BF16_ROWS]).astype(bf16), kz[BF16_ROWS:]], axis=0)
                    s = lax.dot_general(q4, kz, nt_dims, preferred_element_type=f32)
                    p = []
                    for slab, head in enumerate((0, 2, 1, 3)):
                        sc = s[slab * BLOCK:(slab + 1) * BLOCK]
                        s_prev = sc[:, 0:BLOCK] + prev_bias(g * GROUP + head, first)
                        s_cur = sc[:, BLOCK:2 * BLOCK] + bias_cur
                        m = jnp.max(jnp.maximum(s_prev, s_cur), axis=-1, keepdims=True)
                        p.append(jnp.concatenate(
                            [jnp.exp2(s_prev - m), jnp.exp2(s_cur - m)], axis=1).astype(bf16))
                    pv = jnp.dot(jnp.concatenate(p, axis=0), vx, preferred_element_type=f32)
                    for c in range(2):
                        lo = pv[c * BLOCK:(c + 1) * BLOCK]
                        hi = pv[(2 + c) * BLOCK:(3 + c) * BLOCK]
                        num = jnp.where(low_half_blk, lo[:, 0:LANES], hi[:, LANES:2 * LANES])
                        den = jnp.where(low_half_blk, lo[:, LANES:2 * LANES], hi[:, 0:LANES])
                        cols = slice(base + c * LANES, base + (c + 1) * LANES)
                        mix_buf[rows, cols] = (num / den * g_attn[loc, cols]).astype(bf16)
                    yield

        def conv_side():
            u = proj_tiles(lambda c, h, t: c * h, OFF_C, OFF_H)
            u0 = CONV_PAD + r0
            u_buf[u0:u0 + SUB_TILE, :] = u
            conv = (taps[0] * u_buf[u0 - 2:u0 - 2 + SUB_TILE, :]
                    + taps[1] * u_buf[u0 - 1:u0 - 1 + SUB_TILE, :]
                    + taps[2] * u)
            yield
            y_conv = proj_tiles(lambda b, t: b * conv[:, t:t + N_TILE], OFF_B)
            yield
            y_conv = proj_tiles(lambda gc, t: y_conv[:, t:t + N_TILE] * _silu(gc), OFF_GC)
            mix_buf[sub, ATTN_WIDTH:ATTN_WIDTH + CONV_WIDTH] = y_conv.astype(bf16)
            yield

        def output():
            h = x + jnp.dot(mix_buf[sub, :], wout_ref[...], preferred_element_type=f32)
            out_ref[sub, :] = _rms_norm(h, final_gain)
            yield

        return {"q": attention_side(), "c": conv_side(), "o": output()}

    assert tile == 2 * SUB_TILE and SUB_TILE == 4 * BLOCK
    streams = {"A": sub_tile(0), "B": sub_tile(SUB_TILE)}
    for token in TRACE_ORDER.split():
        next(streams[token[0]][token[1]])

    u_buf[0:CONV_PAD, :] = u_buf[tile:tile + CONV_PAD, :]
    kz_buf[:, 0:BLOCK, :] = kz_buf[:, tile:tile + BLOCK, :]
    vx_buf[:, 0:BLOCK, :] = vx_buf[:, tile:tile + BLOCK, :]


def _pack_small_params(norm_g, final_g, conv_w):
    taps = jnp.pad(conv_w.astype(jnp.float32), ((0, 0), (0, D_MODEL - CONV_WIDTH)))
    rows = [norm_g.reshape(1, D_MODEL).astype(jnp.float32), final_g.reshape(1, D_MODEL).astype(jnp.float32), taps]
    pad = jnp.zeros((SMALL_ROWS - 2 - CONV_K, D_MODEL), jnp.float32)
    return jnp.concatenate(rows + [pad], axis=0)


@jax.jit
def kernel(x, norm_g, w_in, sinks, conv_w, w_out, final_g):
    batch, seq, d_model = x.shape
    tile = SEQ_TILE
    assert d_model == D_MODEL and seq % tile == 0 and tile % SUB_TILE == 0 and SUB_TILE % BLOCK == 0
    in_width = w_in.shape[1]
    cos_t, sin_lo, sin_hi = _rope_tables(seq)

    grid_spec = pltpu.PrefetchScalarGridSpec(
        num_scalar_prefetch=1,
        grid=(batch, seq // tile),
        in_specs=[
            pl.BlockSpec((None, tile, D_MODEL), lambda b, j, s: (b, j, 0)),
            pl.BlockSpec((SMALL_ROWS, D_MODEL), lambda b, j, s: (0, 0)),
            pl.BlockSpec((D_MODEL, in_width), lambda b, j, s: (0, 0)),
            pl.BlockSpec((D_MODEL, D_MODEL), lambda b, j, s: (0, 0)),
            pl.BlockSpec((tile, LANES), lambda b, j, s: (j, 0)),
            pl.BlockSpec((tile, LANES), lambda b, j, s: (j, 0)),
            pl.BlockSpec((tile, LANES), lambda b, j, s: (j, 0)),
        ],
        out_specs=pl.BlockSpec((None, tile, D_MODEL), lambda b, j, s: (b, j, 0)),
        scratch_shapes=[
            pltpu.VMEM((tile, 2 * ATTN_WIDTH), jnp.bfloat16),
            pltpu.VMEM((N_KV_HEADS, BLOCK + tile, LANES), jnp.bfloat16),
            pltpu.VMEM((N_KV_HEADS, BLOCK + tile, 2 * LANES), jnp.bfloat16),
            pltpu.VMEM((CONV_PAD + tile, CONV_WIDTH), jnp.float32),
            pltpu.VMEM((tile, D_MODEL), jnp.bfloat16),
        ],
    )
    return pl.pallas_call(
        _layer_kernel,
        grid_spec=grid_spec,
        out_shape=jax.ShapeDtypeStruct(x.shape, x.dtype),
        compiler_params=pltpu.CompilerParams(
            dimension_semantics=("arbitrary", "arbitrary"),
            vmem_limit_bytes=VMEM_LIMIT_BYTES),
        name="hybrid_layer",
    )(sinks.astype(jnp.float32), x, _pack_small_params(norm_g, final_g, conv_w),
      w_in.astype(jnp.bfloat16), w_out.astype(jnp.bfloat16),
      cos_t, sin_lo, sin_hi)
```
